```python
import math, functools
import jax, jax.numpy as jnp
from jax import lax
import numpy as np

D_MODEL = 1024
BATCH = 4
SEQ = 4096
DEPTH = 2
DEC_BATCH = 32
DEC_SEQ = 8
PAST_LEN = 16384
PAGE_SIZE = 128

D_RNN = D_MODEL
LRU_BLOCKS = 16
LRU_BLOCK = D_RNN // LRU_BLOCKS
CONV_W = 4
LRU_C = 8.0
N_HEADS = 8
HEAD_DIM = D_MODEL // (2 * N_HEADS)
V_DIM = 2 * HEAD_DIM
D_ATTN = N_HEADS * V_DIM
SCALE = HEAD_DIM ** -0.5
Q_BLOCK = 128
N_MEM = 256
MEM_HEADS = 4
MEM_HEAD_DIM = 64
D_MEM = MEM_HEADS * MEM_HEAD_DIM
D_FF = ((8 * D_MODEL // 3 + 255) // 256) * 256
EPS = 1e-6
D_IN = 2 * D_RNN + 3 * D_ATTN + 2 * D_MODEL
SPLITS = (D_RNN, 2 * D_RNN, 2 * D_RNN + D_ATTN, 2 * D_RNN + 2 * D_ATTN, 2 * D_RNN + 3 * D_ATTN, 2 * D_RNN + 3 * D_ATTN + D_MODEL)

kernel_name = 'hawk_diffattn_alibi_mem_decoder_step'


def rms_norm(x, g):
    xf = x.astype(jnp.float32)
    y = xf * lax.rsqrt(jnp.mean(xf * xf, axis=-1, keepdims=True) + EPS)
    return (y * g.astype(jnp.float32)).astype(x.dtype)


def alibi_slopes():
    return jnp.asarray(np.array([2.0 ** (-8.0 * (h + 1) / N_HEADS) for h in range(N_HEADS)], dtype=np.float32))


def lambda_init(layer):
    return 0.8 - 0.6 * math.exp(-0.3 * layer)


def diff_lambda(lq1, lk1, lq2, lk2, lam_init):
    f = lambda a: a.astype(jnp.float32)
    return jnp.exp(jnp.sum(f(lq1) * f(lk1))) - jnp.exp(jnp.sum(f(lq2) * f(lk2))) + lam_init


def causal_conv(x, prev, w, b):
    T = x.shape[1]
    xp = jnp.concatenate([prev.astype(x.dtype), x], axis=1)
    y = b
    for j in range(CONV_W):
        y = y + xp[:, j:j + T] * w[j]
    return y, xp[:, -(CONV_W - 1):]


def rg_lru(x, h0, w_rg, b_rg, w_ig, b_ig, lam, start_pos):
    B, T, _ = x.shape
    f32 = jnp.float32
    xf = x.astype(f32)
    xb = xf.reshape(B, T, LRU_BLOCKS, LRU_BLOCK)
    r = jax.nn.sigmoid(jnp.einsum('btnk,nkj->btnj', xb, w_rg.astype(f32)).reshape(B, T, D_RNN) + b_rg.astype(f32))
    i = jax.nn.sigmoid(jnp.einsum('btnk,nkj->btnj', xb, w_ig.astype(f32)).reshape(B, T, D_RNN) + b_ig.astype(f32))
    log_a = -LRU_C * r * jax.nn.softplus(-lam.astype(f32))
    a = jnp.exp(log_a)
    mult = jnp.sqrt(-jnp.expm1(2.0 * log_a))
    reset = ((start_pos + jnp.arange(T)) == 0)[None, :, None]
    mult = jnp.where(reset, 1.0, mult)
    a = jnp.where(reset, 0.0, a)
    b = mult * (i * xf)
    b = b.at[:, 0].add(a[:, 0] * h0.astype(f32))

    def combine(c1, c2):
        a1, b1 = c1
        a2, b2 = c2
        return a1 * a2, a2 * b1 + b2

    _, h = lax.associative_scan(combine, (a, b), axis=1)
    return h.astype(x.dtype), h[:, -1].astype(x.dtype)


def diff_attn_prompt(q, k, v, lam):
    B, S = q.shape[:2]
    nb = S // Q_BLOCK
    slopes = alibi_slopes()
    kpos = jnp.arange(S)
    qb = jnp.moveaxis(q.reshape(B, nb, Q_BLOCK, N_HEADS, 2, HEAD_DIM), 1, 0)

    def block(args):
        qi, bi = args
        qpos = bi * Q_BLOCK + jnp.arange(Q_BLOCK)
        s = jnp.einsum('bqhcd,bkhcd->bchqk', qi, k).astype(jnp.float32) * SCALE
        dist = (qpos[:, None] - kpos[None, :]).astype(jnp.float32)
        bias = jnp.where(dist >= 0, -slopes[:, None, None] * dist, -jnp.inf)
        p = jax.nn.softmax(s + bias, axis=-1)
        pd = p[:, 0] - lam * p[:, 1]
        return jnp.einsum('bhqk,bkhe->bqhe', pd, v.astype(jnp.float32)).astype(q.dtype)

    o = lax.map(block, (qb, jnp.arange(nb)))
    return jnp.moveaxis(o, 0, 1).reshape(B, S, N_HEADS, V_DIM)


def _online_update(carry, s, v):
    m, l, acc = carry
    m_new = jnp.maximum(m, jnp.max(s, axis=-1))
    corr = jnp.exp(m - m_new)
    p = jnp.exp(s - m_new[..., None])
    l = l * corr + jnp.sum(p, axis=-1)
    acc = acc * corr[..., None] + jnp.einsum('bchqk,bkhe->bchqe', p, v.astype(jnp.float32))
    return m_new, l, acc


def diff_attn_sample(q, k, v, lam, cache_k, cache_v, page_table, layer):
    DB, DS = q.shape[:2]
    n_pages = page_table.shape[1]
    past_len = n_pages * PAGE_SIZE
    slopes = alibi_slopes()
    qpos = past_len + jnp.arange(DS)
    s = jnp.einsum('bqhcd,bkhcd->bchqk', q, k).astype(jnp.float32) * SCALE
    dist = (jnp.arange(DS)[:, None] - jnp.arange(DS)[None, :]).astype(jnp.float32)
    s = s + jnp.where(dist >= 0, -slopes[:, None, None] * dist, -jnp.inf)
    m = jnp.max(s, axis=-1)
    p = jnp.exp(s - m[..., None])
    carry = (m, jnp.sum(p, axis=-1), jnp.einsum('bchqk,bkhe->bchqe', p, v.astype(jnp.float32)))

    def page_step(c, pg):
        phys = page_table[:, pg]
        kp = cache_k[layer, phys].reshape(DB, PAGE_SIZE, N_HEADS, 2, HEAD_DIM)
        vp = cache_v[layer, phys]
        kpos = pg * PAGE_SIZE + jnp.arange(PAGE_SIZE)
        sp = jnp.einsum('bqhcd,bkhcd->bchqk', q, kp.astype(q.dtype)).astype(jnp.float32) * SCALE
        d = (qpos[:, None] - kpos[None, :]).astype(jnp.float32)
        sp = sp - slopes[:, None, None] * d
        return _online_update(c, sp, vp), None

    (m, l, acc), _ = lax.scan(page_step, carry, jnp.arange(n_pages))
    o = acc[:, 0] / l[:, 0, ..., None] - lam * (acc[:, 1] / l[:, 1, ..., None])
    return jnp.transpose(o, (0, 2, 1, 3)).astype(q.dtype)


def mixer(xn, conv_prev, h0, start_pos, attn_fn, lw, layer):
    B, T, _ = xn.shape
    z = xn @ lw['w_in']
    x_lru, g_lru, q, k, v, gate_lru, gate_attn = jnp.split(z, SPLITS, axis=-1)
    xc, conv_new = causal_conv(x_lru, conv_prev, lw['conv_w'], lw['conv_b'])
    h, h_last = rg_lru(xc, h0, lw['w_rg'], lw['b_rg'], lw['w_ig'], lw['b_ig'], lw['lru_lambda'], start_pos)
    br_lru = (h * jax.nn.gelu(g_lru)) @ lw['w_br_lru']
    lam_init = lambda_init(layer)
    lam = diff_lambda(lw['lam_q1'], lw['lam_k1'], lw['lam_q2'], lw['lam_k2'], lam_init)
    q = q.reshape(B, T, N_HEADS, 2, HEAD_DIM)
    k = k.reshape(B, T, N_HEADS, 2, HEAD_DIM)
    v = v.reshape(B, T, N_HEADS, V_DIM)
    o = attn_fn(q, k, v, lam)
    o = rms_norm(o, lw['subln']) * (1.0 - lam_init)
    br_attn = o.reshape(B, T, D_ATTN) @ lw['w_br_attn']
    mixed = jax.nn.sigmoid(gate_lru) * br_lru + jax.nn.sigmoid(gate_attn) * br_attn
    return mixed @ lw['w_out'], conv_new, h_last, k.reshape(B, T, N_HEADS, 2 * HEAD_DIM), v


def mem_attn(xn, mk, mv, w_q, w_o):
    B, T, _ = xn.shape
    q = (xn @ w_q).reshape(B, T, MEM_HEADS, MEM_HEAD_DIM)
    s = jnp.einsum('bqhd,bkhd->bhqk', q, mk.astype(q.dtype)).astype(jnp.float32) * (MEM_HEAD_DIM ** -0.5)
    p = jax.nn.softmax(s, axis=-1)
    o = jnp.einsum('bhqk,bkhd->bqhd', p, mv.astype(jnp.float32)).astype(xn.dtype).reshape(B, T, D_MEM)
    return o @ w_o


def mem_kv(mem, lw):
    B = mem.shape[0]
    mkv = rms_norm(mem, lw['norm_memkv']) @ lw['w_mem_kv']
    mk, mv = jnp.split(mkv, 2, axis=-1)
    return mk.reshape(B, N_MEM, MEM_HEADS, MEM_HEAD_DIM), mv.reshape(B, N_MEM, MEM_HEADS, MEM_HEAD_DIM)


def swiglu(xn, w_gate_up, w_down):
    gu = xn @ w_gate_up
    g, u = jnp.split(gu, 2, axis=-1)
    return (jax.nn.silu(g) * u) @ w_down


def layer_fwd(x, mk, mv, conv_prev, h0, start_pos, attn_fn, lw, layer):
    mix, conv_new, h_last, k, v = mixer(rms_norm(x, lw['norm_mix']), conv_prev, h0, start_pos, attn_fn, lw, layer)
    x = x + mix
    x = x + mem_attn(rms_norm(x, lw['norm_mem']), mk, mv, lw['w_mem_q'], lw['w_mem_o'])
    x = x + swiglu(rms_norm(x, lw['norm_ffn']), lw['w_gate_up'], lw['w_down'])
    return x, conv_new, h_last, k, v


def setup_inputs(seed: int = 0) -> dict:
    key = jax.random.key(seed)
    ks = iter(jax.random.split(key, 48))
    f32 = jnp.float32

    def nrm(shape, scale):
        return jax.random.normal(next(ks), shape, f32) * scale

    def gain(shape):
        return 1.0 + 0.05 * jax.random.normal(next(ks), shape, f32)

    n_pages = PAST_LEN // PAGE_SIZE
    n_used = DEC_BATCH * n_pages
    n_pool = n_used + (n_used + 3) // 4
    x_prompt = nrm((BATCH, SEQ, D_MODEL), 1.0)
    mem_prompt = nrm((BATCH, N_MEM, D_MODEL), 1.0)
    x_sample = nrm((DEC_BATCH, DEC_SEQ, D_MODEL), 1.0)
    cache_k = nrm((DEPTH, n_pool, PAGE_SIZE, N_HEADS, 2 * HEAD_DIM), 1.0)
    cache_v = nrm((DEPTH, n_pool, PAGE_SIZE, N_HEADS, V_DIM), 1.0)
    perm = jax.random.permutation(next(ks), n_pool)
    page_table = perm[:n_used].reshape(DEC_BATCH, n_pages).astype(jnp.int32)
    cache_mem_k = nrm((DEPTH, DEC_BATCH, N_MEM, MEM_HEADS, MEM_HEAD_DIM), 1.0)
    cache_mem_v = nrm((DEPTH, DEC_BATCH, N_MEM, MEM_HEADS, MEM_HEAD_DIM), 1.0)
    state_conv = nrm((DEPTH, DEC_BATCH, CONV_W - 1, D_RNN), 1.0)
    state_h = nrm((DEPTH, DEC_BATCH, D_RNN), 0.5)
    u = jax.random.uniform(next(ks), (DEPTH, D_RNN), f32, 0.9, 0.999)
    s = u ** (1.0 / LRU_C)
    lru_lambda = jnp.log(s) - jnp.log1p(-s)
    return {
        'x_prompt': x_prompt, 'mem_prompt': mem_prompt, 'x_sample': x_sample,
        'cache_k': cache_k, 'cache_v': cache_v, 'page_table': page_table,
        'cache_mem_k': cache_mem_k, 'cache_mem_v': cache_mem_v,
        'state_conv': state_conv, 'state_h': state_h,
        'norm_mix': gain((DEPTH, D_MODEL)),
        'w_in': nrm((DEPTH, D_MODEL, D_IN), D_MODEL ** -0.5),
        'conv_w': nrm((DEPTH, CONV_W, D_RNN), CONV_W ** -0.5),
        'conv_b': nrm((DEPTH, D_RNN), 0.02),
        'w_rg': nrm((DEPTH, LRU_BLOCKS, LRU_BLOCK, LRU_BLOCK), LRU_BLOCK ** -0.5),
        'b_rg': nrm((DEPTH, D_RNN), 0.1),
        'w_ig': nrm((DEPTH, LRU_BLOCKS, LRU_BLOCK, LRU_BLOCK), LRU_BLOCK ** -0.5),
        'b_ig': nrm((DEPTH, D_RNN), 0.1),
        'lru_lambda': lru_lambda,
        'lam_q1': nrm((DEPTH, HEAD_DIM), 0.1),
        'lam_k1': nrm((DEPTH, HEAD_DIM), 0.1),
        'lam_q2': nrm((DEPTH, HEAD_DIM), 0.1),
        'lam_k2': nrm((DEPTH, HEAD_DIM), 0.1),
        'subln': gain((DEPTH, V_DIM)),
        'w_br_lru': nrm((DEPTH, D_RNN, D_MODEL), D_RNN ** -0.5),
        'w_br_attn': nrm((DEPTH, D_ATTN, D_MODEL), D_ATTN ** -0.5),
        'w_out': nrm((DEPTH, D_MODEL, D_MODEL), D_MODEL ** -0.5),
        'norm_mem': gain((DEPTH, D_MODEL)),
        'norm_memkv': gain((DEPTH, D_MODEL)),
        'w_mem_q': nrm((DEPTH, D_MODEL, D_MEM), D_MODEL ** -0.5),
        'w_mem_kv': nrm((DEPTH, D_MODEL, 2 * D_MEM), D_MODEL ** -0.5),
        'w_mem_o': nrm((DEPTH, D_MEM, D_MODEL), D_MEM ** -0.5),
        'norm_ffn': gain((DEPTH, D_MODEL)),
        'w_gate_up': nrm((DEPTH, D_MODEL, 2 * D_FF), D_MODEL ** -0.5),
        'w_down': nrm((DEPTH, D_FF, D_MODEL), D_FF ** -0.5),
        'norm_final': gain((D_MODEL,)),
    }


def reference(x_prompt, mem_prompt, x_sample, cache_k, cache_v, page_table, cache_mem_k, cache_mem_v,
              state_conv, state_h, norm_mix, w_in, conv_w, conv_b, w_rg, b_rg, w_ig, b_ig, lru_lambda,
              lam_q1, lam_k1, lam_q2, lam_k2, subln, w_br_lru, w_br_attn, w_out, norm_mem, norm_memkv,
              w_mem_q, w_mem_kv, w_mem_o, norm_ffn, w_gate_up, w_down, norm_final):
    def select(layer):
        return dict(norm_mix=norm_mix[layer], w_in=w_in[layer], conv_w=conv_w[layer], conv_b=conv_b[layer],
                    w_rg=w_rg[layer], b_rg=b_rg[layer], w_ig=w_ig[layer], b_ig=b_ig[layer],
                    lru_lambda=lru_lambda[layer], lam_q1=lam_q1[layer], lam_k1=lam_k1[layer],
                    lam_q2=lam_q2[layer], lam_k2=lam_k2[layer], subln=subln[layer],
                    w_br_lru=w_br_lru[layer], w_br_attn=w_br_attn[layer], w_out=w_out[layer],
                    norm_mem=norm_mem[layer], norm_memkv=norm_memkv[layer], w_mem_q=w_mem_q[layer],
                    w_mem_kv=w_mem_kv[layer], w_mem_o=w_mem_o[layer], norm_ffn=norm_ffn[layer],
                    w_gate_up=w_gate_up[layer], w_down=w_down[layer])

    x = x_prompt
    B = x.shape[0]
    kp_l, vp_l, mkp_l, mvp_l, cp_l, hp_l = [], [], [], [], [], []
    for layer in range(DEPTH):
        lw = select(layer)
        mk, mv = mem_kv(mem_prompt, lw)
        conv0 = jnp.zeros((B, CONV_W - 1, D_RNN), x.dtype)
        h0 = jnp.zeros((B, D_RNN), x.dtype)
        x, conv_new, h_last, k, v = layer_fwd(x, mk, mv, conv0, h0, 0, diff_attn_prompt, lw, layer)
        kp_l.append(k); vp_l.append(v); mkp_l.append(mk); mvp_l.append(mv)
        cp_l.append(conv_new); hp_l.append(h_last)
    y_prompt = rms_norm(x, norm_final)

    past_len = page_table.shape[1] * PAGE_SIZE
    x = x_sample
    ks_l, vs_l, cs_l, hs_l = [], [], [], []
    for layer in range(DEPTH):
        lw = select(layer)
        attn_fn = functools.partial(diff_attn_sample, cache_k=cache_k, cache_v=cache_v,
                                    page_table=page_table, layer=layer)
        x, conv_new, h_last, k, v = layer_fwd(x, cache_mem_k[layer], cache_mem_v[layer], state_conv[layer],
                                              state_h[layer], past_len, attn_fn, lw, layer)
        ks_l.append(k); vs_l.append(v); cs_l.append(conv_new); hs_l.append(h_last)
    y_sample = rms_norm(x, norm_final)

    return (y_prompt, y_sample,
            jnp.stack(kp_l), jnp.stack(vp_l), jnp.stack(mkp_l), jnp.stack(mvp_l), jnp.stack(cp_l), jnp.stack(hp_l),
            jnp.stack(ks_l), jnp.stack(vs_l), jnp.stack(cs_l), jnp.stack(hs_l))
```

```python
import functools
import math

import numpy as np
import jax
import jax.numpy as jnp
from jax import lax
from jax.experimental import pallas as pl
from jax.experimental.pallas import tpu as pltpu

F32 = jnp.float32
BF16 = jnp.bfloat16

D_MODEL = 1024
D_RNN = D_MODEL
LRU_BLOCK = 64
CONV_W = 4
LRU_C = 8.0
N_HEADS = 8
HEAD_DIM = 64
V_DIM = 2 * HEAD_DIM
SCALE = HEAD_DIM ** -0.5
PAGE_SIZE = 128
MEM_HEADS = 4
MEM_HEAD_DIM = 64
D_MEM = MEM_HEADS * MEM_HEAD_DIM
EPS = 1e-6

V7X_VMEM_BYTES = 64 * 1024 * 1024
V7X_MXU_DIM = 256
LANES = 128
SUBLANES = 8

NT_DIMS = (((1,), (1,)), ((), ()))


def _vmem_limit(estimate_bytes):
    return int(min(V7X_VMEM_BYTES - 6 * 1024 * 1024, max(32 * 1024 * 1024, 2 * estimate_bytes)))


def _rms(x, g):
    return x * lax.rsqrt(jnp.mean(x * x, axis=-1, keepdims=True) + EPS) * g


def _full(shape):
    nd = len(shape)
    return pl.BlockSpec(shape, lambda *_: (0,) * nd, pipeline_mode=pl.Buffered(1))


def _norm_proj_kernel(x_ref, g_ref, w_ref, *rest, n_out):
    outs, xn_ref = rest[:n_out], rest[n_out]
    j = pl.program_id(1)

    @pl.when(j == 0)
    def _():
        xn_ref[...] = _rms(x_ref[...], g_ref[...]).astype(BF16)

    y = jnp.dot(xn_ref[...], w_ref[...], preferred_element_type=F32)
    for idx, o_ref in enumerate(outs):
        @pl.when(j == idx)
        def _(o_ref=o_ref):
            o_ref[...] = y


def _norm_proj(x, g, w, n_out, tm, name):
    R, D = x.shape
    wn = w.shape[1] // n_out
    est = 2 * tm * D * 4 + 2 * D * wn * 2 + n_out * 2 * tm * wn * 4 + tm * D * 2 + tm * wn * 4
    return pl.pallas_call(
        functools.partial(_norm_proj_kernel, n_out=n_out),
        out_shape=[jax.ShapeDtypeStruct((R, wn), F32)] * n_out,
        grid=(R // tm, n_out),
        in_specs=[pl.BlockSpec((tm, D), lambda i, j: (i, 0)),
                  pl.BlockSpec((1, D), lambda i, j: (0, 0)),
                  pl.BlockSpec((D, wn), lambda i, j: (0, j))],
        out_specs=[pl.BlockSpec((tm, wn), lambda i, j: (i, 0))] * n_out,
        scratch_shapes=[pltpu.VMEM((tm, D), BF16)],
        compiler_params=pltpu.CompilerParams(dimension_semantics=("arbitrary", "arbitrary"),
                                             vmem_limit_bytes=_vmem_limit(est)),
        name=name,
    )(x, g, w)


def _lru_kernel(xl_ref, gl_ref, cprev_ref, h0_ref, cw_ref, cb_ref, wrg_ref, brg_ref, wig_ref, big_ref,
                lam_ref, hg_ref, cnew_ref, hlast_ref, cp_sc, hc_sc, *, ns, L, start_pos):
    t_blk = pl.program_id(1)
    rows = ns * L
    C = xl_ref.shape[-1]

    @pl.when(t_blk == 0)
    def _():
        cp_sc[...] = cprev_ref[...]
        hc_sc[...] = h0_ref[...]

    x = xl_ref[...]
    tpos = lax.broadcasted_iota(jnp.int32, (rows, C), 0) & (L - 1)

    def per_row(a):
        return jnp.broadcast_to(a, (ns, L, C)).reshape(rows, C)

    prev = cp_sc[...]

    def shifted(k):
        r = pltpu.roll(x, k, axis=0)
        for t0 in range(k):
            r = jnp.where(tpos == t0, per_row(prev[:, CONV_W - 1 - k + t0:CONV_W - k + t0, :]), r)
        return r

    cw = cw_ref[...]
    xc = cb_ref[...] + x * cw[3:4, :]
    for k in range(1, CONV_W):
        xc = xc + shifted(k) * cw[CONV_W - 1 - k:CONV_W - k, :]

    r_parts, i_parts = [], []
    for c in range(C // V7X_MXU_DIM):
        xb = xc[:, c * V7X_MXU_DIM:(c + 1) * V7X_MXU_DIM].astype(BF16)
        r_parts.append(jnp.dot(xb, wrg_ref[c], preferred_element_type=F32))
        i_parts.append(jnp.dot(xb, wig_ref[c], preferred_element_type=F32))
    r = jax.nn.sigmoid(jnp.concatenate(r_parts, axis=1) + brg_ref[...])
    ig = jax.nn.sigmoid(jnp.concatenate(i_parts, axis=1) + big_ref[...])

    nl = -lam_ref[...]
    softplus = jnp.maximum(nl, 0.0) + jnp.log1p(jnp.exp(-jnp.abs(nl)))
    log_a = (-LRU_C) * r * softplus
    a = jnp.exp(log_a)
    mult = jnp.sqrt(-jnp.tanh(log_a) * (a * a + 1.0))
    reset = (start_pos + t_blk * L + tpos) == 0
    mult = jnp.where(reset, 1.0, mult)
    a = jnp.where(reset, 0.0, a)
    b = mult * (ig * xc)
    b = b + jnp.where(tpos == 0, a * per_row(hc_sc[...]), 0.0)

    d = 1
    while d < L:
        valid = tpos >= d
        b = jnp.where(valid, a * pltpu.roll(b, d, axis=0) + b, b)
        if 2 * d < L:
            a = jnp.where(valid, a * pltpu.roll(a, d, axis=0), a)
        d *= 2
    h = b

    hg_ref[...] = (h * jax.nn.gelu(gl_ref[...])).astype(hg_ref.dtype)
    h3 = h.reshape(ns, L, C)
    x3 = x.reshape(ns, L, C)
    hl = h3[:, L - 1:L, :]
    tail = x3[:, L - (CONV_W - 1):L, :]
    hc_sc[...] = hl
    cp_sc[...] = tail
    hlast_ref[...] = hl
    cnew_ref[...] = tail


def _lru(xl, gl, conv_prev, h0, cw, cb, wrg, brg, wig, big, lam, *, ns, L, start_pos, name):
    NS = conv_prev.shape[0]
    R, C = xl.shape
    T = R // NS
    nt = T // L
    rows = ns * L
    est = 2 * 2 * rows * C * 4 + 2 * rows * C * 2 + 12 * rows * C * 4 + 2 * 4 * 256 * 256 * 2
    row_spec = pl.BlockSpec((rows, C), lambda s, t: (s * nt + t, 0))
    vec = pl.BlockSpec((1, C), lambda s, t: (0, 0))
    return pl.pallas_call(
        functools.partial(_lru_kernel, ns=ns, L=L, start_pos=start_pos),
        out_shape=[jax.ShapeDtypeStruct((R, C), BF16),
                   jax.ShapeDtypeStruct((NS, CONV_W - 1, C), F32),
                   jax.ShapeDtypeStruct((NS, 1, C), F32)],
        grid=(NS // ns, nt),
        in_specs=[row_spec, row_spec,
                  pl.BlockSpec((ns, CONV_W - 1, C), lambda s, t: (s, 0, 0)),
                  pl.BlockSpec((ns, 1, C), lambda s, t: (s, 0, 0)),
                  pl.BlockSpec((CONV_W, C), lambda s, t: (0, 0)), vec,
                  pl.BlockSpec(wrg.shape, lambda s, t: (0, 0, 0)), vec,
                  pl.BlockSpec(wig.shape, lambda s, t: (0, 0, 0)), vec, vec],
        out_specs=[row_spec,
                   pl.BlockSpec((ns, CONV_W - 1, C), lambda s, t: (s, 0, 0)),
                   pl.BlockSpec((ns, 1, C), lambda s, t: (s, 0, 0))],
        scratch_shapes=[pltpu.VMEM((ns, CONV_W - 1, C), F32), pltpu.VMEM((ns, 1, C), F32)],
        compiler_params=pltpu.CompilerParams(dimension_semantics=("arbitrary", "arbitrary"),
                                             vmem_limit_bytes=_vmem_limit(est)),
        name=name,
    )(xl, gl, conv_prev, h0, cw, cb, wrg, brg, wig, big, lam)


def _diff_lambda(lamp, lam_init):
    l1 = jnp.sum(lamp[0:1, :] * lamp[1:2, :], axis=-1, keepdims=True)
    l2 = jnp.sum(lamp[2:3, :] * lamp[3:4, :], axis=-1, keepdims=True)
    return jnp.exp(l1) - jnp.exp(l2) + lam_init


def _stack_maps(q):
    lane = lax.broadcasted_iota(jnp.int32, q.shape, 1)
    return jnp.concatenate([jnp.where(lane < HEAD_DIM, q, 0.0), jnp.where(lane >= HEAD_DIM, q, 0.0)], axis=0)


def _attn_prompt_kernel(slopes_ref, q_ref, k_ref, v_ref, lamp_ref, subln_ref, o_ref,
                        kb_sc, vb_sc, m_sc, acc_sc, *, tq, lam_init):
    h = pl.program_id(1)
    i = pl.program_id(2)
    tk = tq

    @pl.when(i == 0)
    def _():
        kb_sc[...] = k_ref[...].astype(BF16)
        vb_sc[:, :V_DIM] = v_ref[...].astype(BF16)
        vb_sc[:, V_DIM:] = jnp.ones((vb_sc.shape[0], V_DIM), BF16)

    qs = _stack_maps(q_ref[...] * SCALE).astype(BF16)
    slope = slopes_ref[h]
    rr = lax.broadcasted_iota(jnp.int32, (2 * tq, tk), 0) & (tq - 1)
    cc = lax.broadcasted_iota(jnp.int32, (2 * tq, tk), 1)
    nsd = (-slope) * (rr - cc).astype(F32)

    m_sc[...] = jnp.full(m_sc.shape, -jnp.inf, F32)
    acc_sc[...] = jnp.zeros(acc_sc.shape, F32)

    def kv_block(j, bias):
        start = pl.multiple_of(j * tk, tk)
        s = lax.dot_general(qs, kb_sc[pl.ds(start, tk), :], NT_DIMS, preferred_element_type=F32) + bias
        m_old = m_sc[...]
        m_new = jnp.maximum(m_old, jnp.max(s, axis=1, keepdims=True))
        p = jnp.exp(s - m_new)
        pv = jnp.dot(p.astype(BF16), vb_sc[pl.ds(start, tk), :], preferred_element_type=F32)
        acc_sc[...] = acc_sc[...] * jnp.exp(m_old - m_new) + pv
        m_sc[...] = m_new

    def body(j, carry):
        kv_block(j, nsd - slope * ((i - j) * tq).astype(F32))
        return carry

    lax.fori_loop(0, i, body, 0)
    kv_block(i, jnp.where(rr >= cc, nsd, -jnp.inf))

    acc = acc_sc[...]
    o = acc[:, :V_DIM] / acc[:, V_DIM:V_DIM + 1]
    lam = _diff_lambda(lamp_ref[...], lam_init)
    od = o[:tq] - lam * o[tq:]
    o_ref[...] = (_rms(od, subln_ref[...]) * (1.0 - lam_init)).astype(o_ref.dtype)


def _attn_prompt(q, k, v, slopes, lamp, subln, *, lam_init, tq, name):
    B, S, _ = q.shape
    est = 2 * 2 * S * 128 * 4 + S * 128 * 2 + S * 256 * 2 + 2 * tq * 256 * 4 + 6 * 2 * tq * tq * 4
    return pl.pallas_call(
        functools.partial(_attn_prompt_kernel, tq=tq, lam_init=lam_init),
        out_shape=jax.ShapeDtypeStruct((B, S, N_HEADS * V_DIM), BF16),
        grid=(B, N_HEADS, S // tq),
        in_specs=[pl.BlockSpec(memory_space=pltpu.SMEM),
                  pl.BlockSpec((None, tq, 2 * HEAD_DIM), lambda b, h, i: (b, i, h)),
                  pl.BlockSpec((None, S, 2 * HEAD_DIM), lambda b, h, i: (b, 0, h)),
                  pl.BlockSpec((None, S, V_DIM), lambda b, h, i: (b, 0, h)),
                  pl.BlockSpec((4, HEAD_DIM), lambda b, h, i: (0, 0)),
                  pl.BlockSpec((1, V_DIM), lambda b, h, i: (0, 0))],
        out_specs=pl.BlockSpec((None, tq, V_DIM), lambda b, h, i: (b, i, h)),
        scratch_shapes=[pltpu.VMEM((S, 2 * HEAD_DIM), BF16), pltpu.VMEM((S, 2 * V_DIM), BF16),
                        pltpu.VMEM((2 * tq, 1), F32), pltpu.VMEM((2 * tq, 2 * V_DIM), F32)],
        compiler_params=pltpu.CompilerParams(dimension_semantics=("arbitrary", "arbitrary", "arbitrary"),
                                             vmem_limit_bytes=_vmem_limit(est)),
        name=name,
    )(slopes, q, k, v, lamp, subln)


def _attn_sample_kernel(pt_ref, q_ref, k_ref, v_ref, slope_ref, lamp_ref, subln_ref, *rest,
                        pb, ds, past_len, lam_init):
    kpages, vpages = rest[:pb], rest[pb:2 * pb]
    o_ref, qs_sc, ko_sc, vo_sc, m_sc, l_sc, acc_sc = rest[2 * pb:]
    p = pl.program_id(1)
    nrow = N_HEADS * 2 * ds
    slope = slope_ref[...]
    row = lax.broadcasted_iota(jnp.int32, (nrow, PAGE_SIZE), 0)
    tok = lax.broadcasted_iota(jnp.int32, (nrow, PAGE_SIZE), 1)
    dq = (row & (ds - 1)) - tok
    nsd = (-slope) * dq.astype(F32)

    def update(kref, vref, bias):
        s_parts = []
        for hd in range(N_HEADS):
            kh = kref[:, hd * V_DIM:(hd + 1) * V_DIM].astype(BF16)
            s_parts.append(lax.dot_general(qs_sc[hd * 2 * ds:(hd + 1) * 2 * ds, :], kh, NT_DIMS,
                                           preferred_element_type=F32))
        s = jnp.concatenate(s_parts, axis=0) + bias
        m_old = m_sc[...]
        m_new = jnp.maximum(m_old, jnp.max(s, axis=1, keepdims=True))
        pr = jnp.exp(s - m_new)
        corr = jnp.exp(m_old - m_new)
        l_sc[...] = l_sc[...] * corr + jnp.sum(pr, axis=1, keepdims=True)
        prb = pr.astype(BF16)
        pv_parts = []
        for hd in range(N_HEADS):
            vh = vref[:, hd * V_DIM:(hd + 1) * V_DIM].astype(BF16)
            pv_parts.append(jnp.dot(prb[hd * 2 * ds:(hd + 1) * 2 * ds, :], vh, preferred_element_type=F32))
        acc_sc[...] = acc_sc[...] * corr + jnp.concatenate(pv_parts, axis=0)
        m_sc[...] = m_new

    @pl.when(p == 0)
    def _():
        q = q_ref[...] * SCALE
        for hd in range(N_HEADS):
            qs_sc[hd * 2 * ds:(hd + 1) * 2 * ds, :] = _stack_maps(q[:, hd * V_DIM:(hd + 1) * V_DIM]).astype(BF16)
        ko_sc[...] = jnp.zeros(ko_sc.shape, F32)
        vo_sc[...] = jnp.zeros(vo_sc.shape, F32)
        ko_sc[0:ds, :] = k_ref[...]
        vo_sc[0:ds, :] = v_ref[...]
        m_sc[...] = jnp.full(m_sc.shape, -jnp.inf, F32)
        l_sc[...] = jnp.zeros(l_sc.shape, F32)
        acc_sc[...] = jnp.zeros(acc_sc.shape, F32)
        update(ko_sc, vo_sc, jnp.where(dq >= 0, nsd, -jnp.inf))

    for ip in range(pb):
        dist0 = (past_len - (p * pb + ip) * PAGE_SIZE).astype(F32)
        update(kpages[ip], vpages[ip], nsd - slope * dist0)

    @pl.when(p == pl.num_programs(1) - 1)
    def _():
        o = acc_sc[...] / l_sc[...]
        lam = _diff_lambda(lamp_ref[...], lam_init)
        for hd in range(N_HEADS):
            r0 = hd * 2 * ds
            od = o[r0:r0 + ds, :] - lam * o[r0 + ds:r0 + 2 * ds, :]
            o_ref[:, hd * V_DIM:(hd + 1) * V_DIM] = (_rms(od, subln_ref[...]) * (1.0 - lam_init)).astype(o_ref.dtype)


def _attn_sample(q, k, v, cache_k, cache_v, page_table, slope_rows, lamp, subln, *, layer, lam_init, pb, name):
    DB, DS, HV = q.shape
    n_pages = page_table.shape[1]
    n_pool = cache_k.shape[0] // 2
    base = layer * n_pool
    nrow = N_HEADS * 2 * DS
    tok_spec = pl.BlockSpec((None, DS, HV), lambda b, p, pt: (b, 0, 0))

    def page_spec(ip):
        return pl.BlockSpec((None, PAGE_SIZE, HV), lambda b, p, pt: (base + pt[b, p * pb + ip], 0, 0))

    est = 2 * 2 * pb * PAGE_SIZE * HV * 4 + 2 * PAGE_SIZE * HV * 4 + 8 * nrow * PAGE_SIZE * 4
    grid_spec = pltpu.PrefetchScalarGridSpec(
        num_scalar_prefetch=1,
        grid=(DB, n_pages // pb),
        in_specs=[tok_spec, tok_spec, tok_spec,
                  pl.BlockSpec((nrow, 1), lambda b, p, pt: (0, 0)),
                  pl.BlockSpec((4, HEAD_DIM), lambda b, p, pt: (0, 0)),
                  pl.BlockSpec((1, V_DIM), lambda b, p, pt: (0, 0))]
                 + [page_spec(ip) for ip in range(pb)] * 2,
        out_specs=tok_spec,
        scratch_shapes=[pltpu.VMEM((nrow, V_DIM), BF16),
                        pltpu.VMEM((PAGE_SIZE, HV), F32), pltpu.VMEM((PAGE_SIZE, HV), F32),
                        pltpu.VMEM((nrow, 1), F32), pltpu.VMEM((nrow, 1), F32), pltpu.VMEM((nrow, V_DIM), F32)])
    return pl.pallas_call(
        functools.partial(_attn_sample_kernel, pb=pb, ds=DS, past_len=n_pages * PAGE_SIZE, lam_init=lam_init),
        out_shape=jax.ShapeDtypeStruct((DB, DS, HV), F32),
        grid_spec=grid_spec,
        compiler_params=pltpu.CompilerParams(dimension_semantics=("arbitrary", "arbitrary"),
                                             vmem_limit_bytes=_vmem_limit(est)),
        name=name,
    )(page_table, q, k, v, slope_rows, lamp, subln, *([cache_k] * pb), *([cache_v] * pb))


def _mix_kernel(hg_ref, o_ref, gl_ref, ga_ref, x_ref, wbl_ref, wba_ref, wout_ref, y_ref):
    bl = jnp.dot(hg_ref[...].astype(BF16), wbl_ref[...], preferred_element_type=F32)
    ba = jnp.dot(o_ref[...].astype(BF16), wba_ref[...], preferred_element_type=F32)
    mixed = jax.nn.sigmoid(gl_ref[...]) * bl + jax.nn.sigmoid(ga_ref[...]) * ba
    y_ref[...] = x_ref[...] + jnp.dot(mixed.astype(BF16), wout_ref[...], preferred_element_type=F32)


def _mix(hg, o, gl, ga, x, wbl, wba, wout, *, tm, name):
    R, D = x.shape
    row = lambda a: pl.BlockSpec((tm, a.shape[1]), lambda i: (i, 0))
    est = 3 * D * D * 2 + 2 * tm * D * (2 + 4 + 4 + 4 + 4 + 4) + 4 * tm * D * 4
    return pl.pallas_call(
        _mix_kernel,
        out_shape=jax.ShapeDtypeStruct((R, D), F32),
        grid=(R // tm,),
        in_specs=[row(hg), row(o), row(gl), row(ga), row(x), _full(wbl.shape), _full(wba.shape), _full(wout.shape)],
        out_specs=row(x),
        compiler_params=pltpu.CompilerParams(dimension_semantics=("arbitrary",), vmem_limit_bytes=_vmem_limit(est)),
        name=name,
    )(hg, o, gl, ga, x, wbl, wba, wout)


def _mem_attn_kernel(x_ref, g_ref, wq_ref, mk_ref, mv_ref, wo_ref, y_ref):
    x = x_ref[...]
    xn = _rms(x, g_ref[...]).astype(BF16)
    q = (jnp.dot(xn, wq_ref[...], preferred_element_type=F32) * (MEM_HEAD_DIM ** -0.5)).astype(BF16)
    mk = mk_ref[...].astype(BF16)
    mv = mv_ref[...].astype(BF16)
    outs = []
    for hd in range(MEM_HEADS):
        sl = slice(hd * MEM_HEAD_DIM, (hd + 1) * MEM_HEAD_DIM)
        s = lax.dot_general(q[:, sl], mk[:, sl], NT_DIMS, preferred_element_type=F32)
        e = jnp.exp(s - jnp.max(s, axis=-1, keepdims=True))
        pr = e / jnp.sum(e, axis=-1, keepdims=True)
        outs.append(jnp.dot(pr.astype(BF16), mv[:, sl], preferred_element_type=F32))
    o = jnp.concatenate(outs, axis=1).astype(BF16)
    y_ref[...] = x + jnp.dot(o, wo_ref[...], preferred_element_type=F32)


def _mem_attn(x, g, wq, mk, mv, wo, *, tm, name):
    NB, T, D = x.shape
    n_mem = mk.shape[1]
    xs = pl.BlockSpec((None, tm, D), lambda b, i: (b, i, 0))
    ms = pl.BlockSpec((None, n_mem, D_MEM), lambda b, i: (b, 0, 0))
    est = 4 * tm * D * 4 + 4 * n_mem * D_MEM * 4 + 2 * D * D_MEM * 2 + 8 * tm * n_mem * 4
    return pl.pallas_call(
        _mem_attn_kernel,
        out_shape=jax.ShapeDtypeStruct((NB, T, D), F32),
        grid=(NB, T // tm),
        in_specs=[xs, pl.BlockSpec((1, D), lambda b, i: (0, 0)), _full(wq.shape), ms, ms, _full(wo.shape)],
        out_specs=xs,
        compiler_params=pltpu.CompilerParams(dimension_semantics=("arbitrary", "arbitrary"),
                                             vmem_limit_bytes=_vmem_limit(est)),
        name=name,
    )(x, g, wq, mk, mv, wo)


def _ffn_kernel(x_ref, g_ref, wg_ref, wu_ref, wd_ref, gf_ref, y_ref, *, final_norm):
    x = x_ref[...]
    xn = _rms(x, g_ref[...]).astype(BF16)
    d_ff = wg_ref.shape[1]
    acc = x
    for c in range(d_ff // V7X_MXU_DIM):
        sl = slice(c * V7X_MXU_DIM, (c + 1) * V7X_MXU_DIM)
        gate = jnp.dot(xn, wg_ref[:, sl], preferred_element_type=F32)
        up = jnp.dot(xn, wu_ref[:, sl], preferred_element_type=F32)
        acc = acc + jnp.dot((jax.nn.silu(gate) * up).astype(BF16), wd_ref[sl, :], preferred_element_type=F32)
    y_ref[...] = _rms(acc, gf_ref[...]) if final_norm else acc


def _ffn(x, g, wg, wu, wd, gf, *, final_norm, tm, name):
    R, D = x.shape
    row = pl.BlockSpec((tm, D), lambda i: (i, 0))
    vec = pl.BlockSpec((1, D), lambda i: (0, 0))
    est = 3 * D * wg.shape[1] * 2 + 4 * tm * D * 4 + 6 * tm * D * 4
    return pl.pallas_call(
        functools.partial(_ffn_kernel, final_norm=final_norm),
        out_shape=jax.ShapeDtypeStruct((R, D), F32),
        grid=(R // tm,),
        in_specs=[row, vec, _full(wg.shape), _full(wu.shape), _full(wd.shape), vec],
        out_specs=row,
        compiler_params=pltpu.CompilerParams(dimension_semantics=("arbitrary",), vmem_limit_bytes=_vmem_limit(est)),
        name=name,
    )(x, g, wg, wu, wd, gf)


def _block_diag_tiles(w):
    nb = w.shape[0]
    per = V7X_MXU_DIM // LRU_BLOCK
    eye = jnp.eye(per, dtype=w.dtype)
    w4 = w.reshape(nb // per, per, LRU_BLOCK, LRU_BLOCK)
    t = jnp.einsum('gakj,ab->gakbj', w4, eye)
    return t.reshape(nb // per, V7X_MXU_DIM, V7X_MXU_DIM).astype(BF16)


def _row(v):
    return v.reshape(1, -1).astype(F32)


def kernel(x_prompt, mem_prompt, x_sample, cache_k, cache_v, page_table, cache_mem_k, cache_mem_v, state_conv, state_h, norm_mix, w_in, conv_w, conv_b, w_rg, b_rg, w_ig, b_ig, lru_lambda, lam_q1, lam_k1, lam_q2, lam_k2, subln, w_br_lru, w_br_attn, w_out, norm_mem, norm_memkv, w_mem_q, w_mem_kv, w_mem_o, norm_ffn, w_gate_up, w_down, norm_final):
    depth = w_in.shape[0]
    B, S, D = x_prompt.shape
    DB, DS, _ = x_sample.shape
    n_mem = mem_prompt.shape[1]
    n_pool = cache_k.shape[1]
    n_pages = page_table.shape[1]
    d_ff = w_down.shape[1]
    HV = N_HEADS * V_DIM

    slopes_np = np.array([2.0 ** (-8.0 * (h + 1) / N_HEADS) for h in range(N_HEADS)], dtype=np.float32)
    slopes = jnp.asarray(slopes_np)
    slope_rows = jnp.asarray(np.repeat(slopes_np, 2 * DS).reshape(-1, 1))
    ck = cache_k.reshape(depth * n_pool, PAGE_SIZE, HV)
    cv = cache_v.reshape(depth * n_pool, PAGE_SIZE, HV)
    mem2d = mem_prompt.reshape(B * n_mem, D)
    gfin = _row(norm_final)

    xp = x_prompt.reshape(B * S, D)
    xs = x_sample.reshape(DB * DS, D)
    zc_p = jnp.zeros((B, CONV_W - 1, D_RNN), F32)
    zh_p = jnp.zeros((B, 1, D_RNN), F32)
    outs_p = {n: [] for n in ('k', 'v', 'mk', 'mv', 'c', 'h')}
    outs_s = {n: [] for n in ('k', 'v', 'c', 'h')}

    for layer in range(depth):
        lam_init = 0.8 - 0.6 * math.exp(-0.3 * layer)
        last = layer == depth - 1
        w_in_b = w_in[layer].astype(BF16)
        wrg = _block_diag_tiles(w_rg[layer])
        wig = _block_diag_tiles(w_ig[layer])
        lamp = jnp.stack([lam_q1[layer], lam_k1[layer], lam_q2[layer], lam_k2[layer]]).astype(F32)
        wbl, wba, wout = (w_br_lru[layer].astype(BF16), w_br_attn[layer].astype(BF16), w_out[layer].astype(BF16))
        wmq, wmo = w_mem_q[layer].astype(BF16), w_mem_o[layer].astype(BF16)
        wgu = w_gate_up[layer].astype(BF16)
        wg, wu, wd = wgu[:, :d_ff], wgu[:, d_ff:], w_down[layer].astype(BF16)
        lru_args = (conv_w[layer].astype(F32), _row(conv_b[layer]), wrg, _row(b_rg[layer]), wig, _row(b_ig[layer]),
                    _row(lru_lambda[layer]))

        def trunk(x2d, nb, t, mk, mv, conv_prev, h0, start_pos, attn_fn, tm, tm_mem, ns, L, tag):
            xl, gl, q, k, v, gtl, gta = _norm_proj(x2d, _row(norm_mix[layer]), w_in_b, 7, tm, f"in_proj_{tag}")
            hg, c_new, h_last = _lru(xl, gl, conv_prev, h0, *lru_args, ns=ns, L=L, start_pos=start_pos,
                                     name=f"lru_{tag}")
            o = attn_fn(q.reshape(nb, t, HV), k.reshape(nb, t, HV), v.reshape(nb, t, HV)).reshape(nb * t, HV)
            x2 = _mix(hg, o, gtl, gta, x2d, wbl, wba, wout, tm=tm, name=f"mix_{tag}")
            x3 = _mem_attn(x2.reshape(nb, t, D), _row(norm_mem[layer]), wmq, mk, mv, wmo, tm=tm_mem,
                           name=f"mem_attn_{tag}").reshape(nb * t, D)
            x4 = _ffn(x3, _row(norm_ffn[layer]), wg, wu, wd, gfin, final_norm=last, tm=tm, name=f"ffn_{tag}")
            return x4, c_new, h_last.reshape(nb, D_RNN), k.reshape(nb, t, N_HEADS, V_DIM), v.reshape(nb, t, N_HEADS, V_DIM)

        mk2, mv2 = _norm_proj(mem2d, _row(norm_memkv[layer]), w_mem_kv[layer].astype(BF16), 2, 256, f"mem_kv_{layer}")
        mk_p = mk2.reshape(B, n_mem, D_MEM)
        mv_p = mv2.reshape(B, n_mem, D_MEM)
        attn_p = functools.partial(_attn_prompt, slopes=slopes, lamp=lamp, subln=_row(subln[layer]),
                                   lam_init=lam_init, tq=512, name=f"attn_prompt_{layer}")
        xp, c_new, h_last, k, v = trunk(xp, B, S, mk_p, mv_p, zc_p, zh_p, 0, attn_p, 512, 512, 1, 256, f"p{layer}")
        outs_p['k'].append(k); outs_p['v'].append(v)
        outs_p['mk'].append(mk_p.reshape(B, n_mem, MEM_HEADS, MEM_HEAD_DIM))
        outs_p['mv'].append(mv_p.reshape(B, n_mem, MEM_HEADS, MEM_HEAD_DIM))
        outs_p['c'].append(c_new); outs_p['h'].append(h_last)

        attn_s = functools.partial(_attn_sample, cache_k=ck, cache_v=cv, page_table=page_table,
                                   slope_rows=slope_rows, lamp=lamp, subln=_row(subln[layer]), layer=layer,
                                   lam_init=lam_init, pb=4, name=f"attn_sample_{layer}")
        xs, c_new, h_last, k, v = trunk(xs, DB, DS, cache_mem_k[layer].reshape(DB, n_mem, D_MEM),
                                        cache_mem_v[layer].reshape(DB, n_mem, D_MEM), state_conv[layer],
                                        state_h[layer].reshape(DB, 1, D_RNN), n_pages * PAGE_SIZE, attn_s,
                                        DB * DS, DS, DB, DS, f"s{layer}")
        outs_s['k'].append(k); outs_s['v'].append(v); outs_s['c'].append(c_new); outs_s['h'].append(h_last)

    st = jnp.stack
    return (xp.reshape(B, S, D), xs.reshape(DB, DS, D),
            st(outs_p['k']), st(outs_p['v']), st(outs_p['mk']), st(outs_p['mv']), st(outs_p['c']), st(outs_p['h']),
            st(outs_s['k']), st(outs_s['v']), st(outs_s['c']), st(outs_s['h']))
```

```python
import functools
import math

import numpy as np
import jax
import jax.numpy as jnp
from jax import lax
from jax.experimental import pallas as pl
from jax.experimental.pallas import tpu as pltpu

F32 = jnp.float32
BF16 = jnp.bfloat16

D_MODEL = 1024
D_RNN = D_MODEL
LRU_BLOCK = 64
CONV_W = 4
LRU_C = 8.0
N_HEADS = 8
HEAD_DIM = 64
V_DIM = 2 * HEAD_DIM
SCALE = HEAD_DIM ** -0.5
PAGE_SIZE = 128
MEM_HEADS = 4
MEM_HEAD_DIM = 64
D_MEM = MEM_HEADS * MEM_HEAD_DIM
EPS = 1e-6

V7X_VMEM_BYTES = 64 * 1024 * 1024
V7X_MXU_DIM = 256
LANES = 128
SUBLANES = 8

NT_DIMS = (((1,), (1,)), ((), ()))


def _vmem_limit(estimate_bytes):
    return int(min(V7X_VMEM_BYTES - 6 * 1024 * 1024, max(32 * 1024 * 1024, 2 * estimate_bytes)))


def _rms(x, g):
    return x * lax.rsqrt(jnp.mean(x * x, axis=-1, keepdims=True) + EPS) * g


def _full(shape):
    nd = len(shape)
    return pl.BlockSpec(shape, lambda *_: (0,) * nd, pipeline_mode=pl.Buffered(1))


def _norm_proj_kernel(x_ref, g_ref, w_ref, *outs):
    xn = _rms(x_ref[...], g_ref[...]).astype(BF16)
    wn = outs[0].shape[1]
    for idx, o_ref in enumerate(outs):
        y = jnp.dot(xn, w_ref[:, idx * wn:(idx + 1) * wn], preferred_element_type=F32)
        o_ref[...] = y.astype(o_ref.dtype)


def _norm_proj(x, g, w, out_dtypes, tm, name):
    R, D = x.shape
    n_out = len(out_dtypes)
    wn = w.shape[1] // n_out
    est = 2 * tm * D * 4 + D * w.shape[1] * 2 + n_out * 2 * tm * wn * 4 + tm * D * 2 + 2 * tm * wn * 4
    return pl.pallas_call(
        _norm_proj_kernel,
        out_shape=[jax.ShapeDtypeStruct((R, wn), dt) for dt in out_dtypes],
        grid=(R // tm,),
        in_specs=[pl.BlockSpec((tm, D), lambda i: (i, 0)),
                  pl.BlockSpec((1, D), lambda i: (0, 0)),
                  _full(w.shape)],
        out_specs=[pl.BlockSpec((tm, wn), lambda i: (i, 0))] * n_out,
        compiler_params=pltpu.CompilerParams(dimension_semantics=("arbitrary",),
                                             vmem_limit_bytes=_vmem_limit(est)),
        name=name,
    )(x, g, w)


def _lru_kernel(xl_ref, gl_ref, cprev_ref, h0_ref, cw_ref, cb_ref, wrg_ref, brg_ref, wig_ref, big_ref,
                lam_ref, hg_ref, cnew_ref, hlast_ref, cp_sc, hc_sc, *, ns, L, start_pos):
    t_blk = pl.program_id(1)
    rows = ns * L
    C = xl_ref.shape[-1]

    @pl.when(t_blk == 0)
    def _():
        cp_sc[...] = cprev_ref[...]
        hc_sc[...] = h0_ref[...]

    x = xl_ref[...]
    tpos = lax.broadcasted_iota(jnp.int32, (rows, C), 0) & (L - 1)

    def per_row(a):
        return jnp.broadcast_to(a, (ns, L, C)).reshape(rows, C)

    prev = cp_sc[...]

    def shifted(k):
        r = pltpu.roll(x, k, axis=0)
        for t0 in range(k):
            r = jnp.where(tpos == t0, per_row(prev[:, CONV_W - 1 - k + t0:CONV_W - k + t0, :]), r)
        return r

    cw = cw_ref[...]
    xc = cb_ref[...] + x * cw[3:4, :]
    for k in range(1, CONV_W):
        xc = xc + shifted(k) * cw[CONV_W - 1 - k:CONV_W - k, :]

    r_parts, i_parts = [], []
    for c in range(C // V7X_MXU_DIM):
        xb = xc[:, c * V7X_MXU_DIM:(c + 1) * V7X_MXU_DIM].astype(BF16)
        r_parts.append(jnp.dot(xb, wrg_ref[c], preferred_element_type=F32))
        i_parts.append(jnp.dot(xb, wig_ref[c], preferred_element_type=F32))
    r = jax.nn.sigmoid(jnp.concatenate(r_parts, axis=1) + brg_ref[...])
    ig = jax.nn.sigmoid(jnp.concatenate(i_parts, axis=1) + big_ref[...])

    nl = -lam_ref[...]
    softplus = jnp.maximum(nl, 0.0) + jnp.log1p(jnp.exp(-jnp.abs(nl)))
    log_a = (-LRU_C) * r * softplus
    a = jnp.exp(log_a)
    mult = jnp.sqrt(-jnp.tanh(log_a) * (a * a + 1.0))
    reset = (start_pos + t_blk * L + tpos) == 0
    mult = jnp.where(reset, 1.0, mult)
    a = jnp.where(reset, 0.0, a)
    b = mult * (ig * xc)
    b = b + jnp.where(tpos == 0, a * per_row(hc_sc[...]), 0.0)

    d = 1
    while d < L:
        valid = tpos >= d
        b = jnp.where(valid, a * pltpu.roll(b, d, axis=0) + b, b)
        if 2 * d < L:
            a = jnp.where(valid, a * pltpu.roll(a, d, axis=0), a)
        d *= 2
    h = b

    hg_ref[...] = (h * jax.nn.gelu(gl_ref[...])).astype(hg_ref.dtype)
    h3 = h.reshape(ns, L, C)
    x3 = x.reshape(ns, L, C)
    hl = h3[:, L - 1:L, :]
    tail = x3[:, L - (CONV_W - 1):L, :]
    hc_sc[...] = hl
    cp_sc[...] = tail
    hlast_ref[...] = hl
    cnew_ref[...] = tail


def _lru(xl, gl, conv_prev, h0, cw, cb, wrg, brg, wig, big, lam, *, ns, L, start_pos, name):
    NS = conv_prev.shape[0]
    R, C = xl.shape
    T = R // NS
    nt = T // L
    rows = ns * L
    est = 2 * 2 * rows * C * 4 + 2 * rows * C * 2 + 12 * rows * C * 4 + 2 * 4 * 256 * 256 * 2
    row_spec = pl.BlockSpec((rows, C), lambda s, t: (s * nt + t, 0))
    vec = pl.BlockSpec((1, C), lambda s, t: (0, 0))
    return pl.pallas_call(
        functools.partial(_lru_kernel, ns=ns, L=L, start_pos=start_pos),
        out_shape=[jax.ShapeDtypeStruct((R, C), BF16),
                   jax.ShapeDtypeStruct((NS, CONV_W - 1, C), F32),
                   jax.ShapeDtypeStruct((NS, 1, C), F32)],
        grid=(NS // ns, nt),
        in_specs=[row_spec, row_spec,
                  pl.BlockSpec((ns, CONV_W - 1, C), lambda s, t: (s, 0, 0)),
                  pl.BlockSpec((ns, 1, C), lambda s, t: (s, 0, 0)),
                  pl.BlockSpec((CONV_W, C), lambda s, t: (0, 0)), vec,
                  pl.BlockSpec(wrg.shape, lambda s, t: (0, 0, 0)), vec,
                  pl.BlockSpec(wig.shape, lambda s, t: (0, 0, 0)), vec, vec],
        out_specs=[row_spec,
                   pl.BlockSpec((ns, CONV_W - 1, C), lambda s, t: (s, 0, 0)),
                   pl.BlockSpec((ns, 1, C), lambda s, t: (s, 0, 0))],
        scratch_shapes=[pltpu.VMEM((ns, CONV_W - 1, C), F32), pltpu.VMEM((ns, 1, C), F32)],
        compiler_params=pltpu.CompilerParams(dimension_semantics=("arbitrary", "arbitrary"),
                                             vmem_limit_bytes=_vmem_limit(est)),
        name=name,
    )(xl, gl, conv_prev, h0, cw, cb, wrg, brg, wig, big, lam)


def _diff_lambda(lamp, lam_init):
    l1 = jnp.sum(lamp[0:1, :] * lamp[1:2, :], axis=-1, keepdims=True)
    l2 = jnp.sum(lamp[2:3, :] * lamp[3:4, :], axis=-1, keepdims=True)
    return jnp.exp(l1) - jnp.exp(l2) + lam_init


def _stack_maps(q):
    lane = lax.broadcasted_iota(jnp.int32, q.shape, 1)
    return jnp.concatenate([jnp.where(lane < HEAD_DIM, q, 0.0), jnp.where(lane >= HEAD_DIM, q, 0.0)], axis=0)


POS_DIGITS = (256, 16, 1)
ATTN_COL_CHUNK = V7X_MXU_DIM
VT_ROWS = V_DIM + 16


def _attn_prompt_kernel(slopes_ref, q_ref, k_ref, v_ref, lamp_ref, subln_ref, o_ref,
                        qa_sc, ka_sc, vt_sc, m_sc, acc_sc, *, tq, hp, lam_init):
    i = pl.program_id(2)
    tk = tq
    S = k_ref.shape[0]
    dk = 2 * HEAD_DIM

    @pl.when(i == 0)
    def _():
        pos = lax.broadcasted_iota(jnp.int32, (S, dk), 0)
        lane = lax.broadcasted_iota(jnp.int32, (S, dk), 1)
        digit = jnp.where(lane == 0, pos >> 8, jnp.where(lane == 1, (pos >> 4) & 15, jnp.where(lane == 2, pos & 15, 0)))
        digit = digit.astype(F32).astype(BF16)
        for g in range(hp):
            ka_sc[g, :, :dk] = k_ref[:, g * dk:(g + 1) * dk].astype(BF16)
            ka_sc[g, :, dk:] = digit
            vt = v_ref[:, g * V_DIM:(g + 1) * V_DIM].T
            for jb in range(S // tk):
                vt_sc[g, jb, :V_DIM, :] = vt[:, jb * tk:(jb + 1) * tk].astype(BF16)
                vt_sc[g, jb, V_DIM:, :] = jnp.ones((VT_ROWS - V_DIM, tk), BF16)

    lane = lax.broadcasted_iota(jnp.int32, (2 * tq, dk), 1)
    for g in range(hp):
        slope = slopes_ref[pl.program_id(1) * hp + g]
        qa_sc[g, :, :dk] = _stack_maps(q_ref[:, g * dk:(g + 1) * dk].astype(F32) * SCALE).astype(BF16)
        posw = jnp.where(lane == 0, POS_DIGITS[0] * slope,
                         jnp.where(lane == 1, POS_DIGITS[1] * slope, jnp.where(lane == 2, POS_DIGITS[2] * slope, 0.0)))
        qa_sc[g, :, dk:] = posw.astype(BF16)
    m_sc[...] = jnp.full(m_sc.shape, -jnp.inf, F32)
    acc_sc[...] = jnp.zeros(acc_sc.shape, F32)

    def kv_block(j, diag):
        start = pl.multiple_of(j * tk, tk)
        chains = [(g, c0) for g in range(hp) for c0 in range(0, 2 * tq, ATTN_COL_CHUNK)]
        probs = []
        for g, c0 in chains:
            cols = slice(c0, c0 + ATTN_COL_CHUNK)
            s = lax.dot_general(ka_sc[g, pl.ds(start, tk), :], qa_sc[g, cols, :], NT_DIMS,
                                preferred_element_type=F32)
            if diag:
                qcol = (lax.broadcasted_iota(jnp.int32, s.shape, 1) + c0) & (tq - 1)
                s = jnp.where(qcol >= lax.broadcasted_iota(jnp.int32, s.shape, 0), s, -jnp.inf)
            m_old = m_sc[g, :, cols]
            m_new = jnp.maximum(m_old, jnp.max(s, axis=0, keepdims=True))
            m_sc[g, :, cols] = m_new
            probs.append((jnp.exp(s - m_new).astype(BF16), jnp.exp(m_old - m_new)))
        for (g, c0), (p, corr) in zip(chains, probs):
            cols = slice(c0, c0 + ATTN_COL_CHUNK)
            pv = jnp.dot(vt_sc[g, j], p, preferred_element_type=F32)
            acc_sc[g, :, cols] = acc_sc[g, :, cols] * corr + pv

    def body(j, carry):
        kv_block(j, False)
        return carry

    lax.fori_loop(0, i, body, 0)
    kv_block(i, True)

    lam = _diff_lambda(lamp_ref[...], lam_init)
    for g in range(hp):
        acc = acc_sc[g]
        o = acc[:V_DIM, :] / acc[V_DIM:V_DIM + 1, :]
        od = o[:, :tq] - lam * o[:, tq:]
        y = od * lax.rsqrt(jnp.mean(od * od, axis=0, keepdims=True) + EPS) * subln_ref[...] * (1.0 - lam_init)
        o_ref[:, g * V_DIM:(g + 1) * V_DIM] = y.T.astype(o_ref.dtype)


def _attn_prompt(q, k, v, slopes, lamp, subln_col, *, lam_init, tq, hp, name):
    B, S, _ = q.shape
    assert S <= 256 * POS_DIGITS[0] and (2 * tq) % ATTN_COL_CHUNK == 0 and S % tq == 0 and N_HEADS % hp == 0
    est = (2 * 2 * S * hp * 128 * 4 + hp * (S * 256 * 2 + S * VT_ROWS * 2 + 2 * tq * 256 * 2 + VT_ROWS * 2 * tq * 4)
           + 2 * hp * tq * 2 * tq * 4 + S * 128 * 4)
    return pl.pallas_call(
        functools.partial(_attn_prompt_kernel, tq=tq, hp=hp, lam_init=lam_init),
        out_shape=jax.ShapeDtypeStruct((B, S, N_HEADS * V_DIM), BF16),
        grid=(B, N_HEADS // hp, S // tq),
        in_specs=[pl.BlockSpec(memory_space=pltpu.SMEM),
                  pl.BlockSpec((None, tq, hp * 2 * HEAD_DIM), lambda b, h, i: (b, i, h)),
                  pl.BlockSpec((None, S, hp * 2 * HEAD_DIM), lambda b, h, i: (b, 0, h)),
                  pl.BlockSpec((None, S, hp * V_DIM), lambda b, h, i: (b, 0, h)),
                  pl.BlockSpec((4, HEAD_DIM), lambda b, h, i: (0, 0)),
                  pl.BlockSpec((V_DIM, 1), lambda b, h, i: (0, 0))],
        out_specs=pl.BlockSpec((None, tq, hp * V_DIM), lambda b, h, i: (b, i, h)),
        scratch_shapes=[pltpu.VMEM((hp, 2 * tq, 4 * HEAD_DIM), BF16),
                        pltpu.VMEM((hp, S, 4 * HEAD_DIM), BF16), pltpu.VMEM((hp, S // tq, VT_ROWS, tq), BF16),
                        pltpu.VMEM((hp, 1, 2 * tq), F32), pltpu.VMEM((hp, VT_ROWS, 2 * tq), F32)],
        compiler_params=pltpu.CompilerParams(dimension_semantics=("arbitrary", "arbitrary", "arbitrary"),
                                             vmem_limit_bytes=_vmem_limit(est)),
        name=name,
    )(slopes, q, k, v, lamp, subln_col)


def _attn_sample_kernel(pt_ref, q_ref, k_ref, v_ref, slope_ref, lamp_ref, subln_ref, *rest,
                        pb, ds, past_len, lam_init):
    kpages, vpages = rest[:pb], rest[pb:2 * pb]
    o_ref, qs_sc, ko_sc, vo_sc, m_sc, l_sc, acc_sc = rest[2 * pb:]
    p = pl.program_id(1)
    nrow = N_HEADS * 2 * ds
    slope = slope_ref[...]
    row = lax.broadcasted_iota(jnp.int32, (nrow, PAGE_SIZE), 0)
    tok = lax.broadcasted_iota(jnp.int32, (nrow, PAGE_SIZE), 1)
    dq = (row & (ds - 1)) - tok
    nsd = (-slope) * dq.astype(F32)

    def head_rows(ref, hd):
        return ref[pl.ds(hd, PAGE_SIZE, stride=N_HEADS), :].astype(BF16)

    def update(krefs, vrefs, biases):
        rows = lambda hd: slice(hd * 2 * ds, (hd + 1) * 2 * ds)
        s_heads = []
        for hd in range(N_HEADS):
            kh = jnp.concatenate([head_rows(r, hd) for r in krefs], axis=0)
            s_heads.append(lax.dot_general(qs_sc[rows(hd), :], kh, NT_DIMS, preferred_element_type=F32))
        s = jnp.concatenate(s_heads, axis=0) + jnp.concatenate(biases, axis=1)
        m_old = m_sc[...]
        m_new = jnp.maximum(m_old, jnp.max(s, axis=1, keepdims=True))
        pr = jnp.exp(s - m_new)
        corr = jnp.exp(m_old - m_new)
        l_sc[...] = l_sc[...] * corr + jnp.sum(pr, axis=1, keepdims=True)
        prb = pr.astype(BF16)
        pv_heads = []
        for hd in range(N_HEADS):
            vh = jnp.concatenate([head_rows(r, hd) for r in vrefs], axis=0)
            pv_heads.append(jnp.dot(prb[rows(hd), :], vh, preferred_element_type=F32))
        acc_sc[...] = acc_sc[...] * corr + jnp.concatenate(pv_heads, axis=0)
        m_sc[...] = m_new

    @pl.when(p == 0)
    def _():
        q = q_ref[...].astype(F32) * SCALE
        k_own, v_own = k_ref[...], v_ref[...]
        ko_sc[...] = jnp.zeros(ko_sc.shape, F32)
        vo_sc[...] = jnp.zeros(vo_sc.shape, F32)
        for hd in range(N_HEADS):
            cols = slice(hd * V_DIM, (hd + 1) * V_DIM)
            qs_sc[hd * 2 * ds:(hd + 1) * 2 * ds, :] = _stack_maps(q[:, cols]).astype(BF16)
            ko_sc[pl.ds(hd, ds, stride=N_HEADS), :] = k_own[:, cols]
            vo_sc[pl.ds(hd, ds, stride=N_HEADS), :] = v_own[:, cols]
        m_sc[...] = jnp.full(m_sc.shape, -jnp.inf, F32)
        l_sc[...] = jnp.zeros(l_sc.shape, F32)
        acc_sc[...] = jnp.zeros(acc_sc.shape, F32)
        update([ko_sc], [vo_sc], [jnp.where(dq >= 0, nsd, -jnp.inf)])

    dist0 = [(past_len - (p * pb + ip) * PAGE_SIZE).astype(F32) for ip in range(pb)]
    update(kpages, vpages, [nsd - slope * d0 for d0 in dist0])

    @pl.when(p == pl.num_programs(1) - 1)
    def _():
        o = acc_sc[...] / l_sc[...]
        lam = _diff_lambda(lamp_ref[...], lam_init)
        for hd in range(N_HEADS):
            r0 = hd * 2 * ds
            od = o[r0:r0 + ds, :] - lam * o[r0 + ds:r0 + 2 * ds, :]
            o_ref[:, hd * V_DIM:(hd + 1) * V_DIM] = (_rms(od, subln_ref[...]) * (1.0 - lam_init)).astype(o_ref.dtype)


def _attn_sample(q, k, v, cache_k, cache_v, page_table, slope_rows, lamp, subln, *, layer, lam_init, pb, name):
    DB, DS, HV = q.shape
    n_pages = page_table.shape[1]
    nrow = N_HEADS * 2 * DS
    prow = PAGE_SIZE * N_HEADS
    tok_spec = pl.BlockSpec((None, DS, HV), lambda b, p, pt: (b, 0, 0))

    def page_spec(ip):
        return pl.BlockSpec((None, None, prow, V_DIM), lambda b, p, pt: (layer, pt[b, p * pb + ip], 0, 0))

    est = 2 * 2 * pb * PAGE_SIZE * HV * 4 + 2 * PAGE_SIZE * HV * 4 + 8 * nrow * pb * PAGE_SIZE * 4
    grid_spec = pltpu.PrefetchScalarGridSpec(
        num_scalar_prefetch=1,
        grid=(DB, n_pages // pb),
        in_specs=[tok_spec, tok_spec, tok_spec,
                  pl.BlockSpec((nrow, 1), lambda b, p, pt: (0, 0)),
                  pl.BlockSpec((4, HEAD_DIM), lambda b, p, pt: (0, 0)),
                  pl.BlockSpec((1, V_DIM), lambda b, p, pt: (0, 0))]
                 + [page_spec(ip) for ip in range(pb)] * 2,
        out_specs=tok_spec,
        scratch_shapes=[pltpu.VMEM((nrow, V_DIM), BF16),
                        pltpu.VMEM((prow, V_DIM), F32), pltpu.VMEM((prow, V_DIM), F32),
                        pltpu.VMEM((nrow, 1), F32), pltpu.VMEM((nrow, 1), F32), pltpu.VMEM((nrow, V_DIM), F32)])
    return pl.pallas_call(
        functools.partial(_attn_sample_kernel, pb=pb, ds=DS, past_len=n_pages * PAGE_SIZE, lam_init=lam_init),
        out_shape=jax.ShapeDtypeStruct((DB, DS, HV), F32),
        grid_spec=grid_spec,
        compiler_params=pltpu.CompilerParams(dimension_semantics=("arbitrary", "arbitrary"),
                                             vmem_limit_bytes=_vmem_limit(est)),
        name=name,
    )(page_table, q, k, v, slope_rows, lamp, subln, *([cache_k] * pb), *([cache_v] * pb))


def _mix_kernel(hg_ref, o_ref, gl_ref, ga_ref, x_ref, wbl_ref, wba_ref, wout_ref, y_ref):
    bl = jnp.dot(hg_ref[...].astype(BF16), wbl_ref[...], preferred_element_type=F32)
    ba = jnp.dot(o_ref[...].astype(BF16), wba_ref[...], preferred_element_type=F32)
    mixed = jax.nn.sigmoid(gl_ref[...]) * bl + jax.nn.sigmoid(ga_ref[...]) * ba
    y_ref[...] = x_ref[...] + jnp.dot(mixed.astype(BF16), wout_ref[...], preferred_element_type=F32)


def _mix(hg, o, gl, ga, x, wbl, wba, wout, *, tm, name):
    R, D = x.shape
    row = lambda a: pl.BlockSpec((tm, a.shape[1]), lambda i: (i, 0))
    est = 3 * D * D * 2 + 2 * tm * D * (2 + 4 + 4 + 4 + 4 + 4) + 4 * tm * D * 4
    return pl.pallas_call(
        _mix_kernel,
        out_shape=jax.ShapeDtypeStruct((R, D), F32),
        grid=(R // tm,),
        in_specs=[row(hg), row(o), row(gl), row(ga), row(x), _full(wbl.shape), _full(wba.shape), _full(wout.shape)],
        out_specs=row(x),
        compiler_params=pltpu.CompilerParams(dimension_semantics=("arbitrary",), vmem_limit_bytes=_vmem_limit(est)),
        name=name,
    )(hg, o, gl, ga, x, wbl, wba, wout)


def _mem_attn_kernel(x_ref, g_ref, wq_ref, mk_ref, mv_ref, wo_ref, y_ref):
    x = x_ref[...]
    xn = _rms(x, g_ref[...]).astype(BF16)
    q = (jnp.dot(xn, wq_ref[...], preferred_element_type=F32) * (MEM_HEAD_DIM ** -0.5)).astype(BF16)
    mk = mk_ref[...].astype(BF16)
    mv = mv_ref[...].astype(BF16)
    outs = []
    for hd in range(MEM_HEADS):
        sl = slice(hd * MEM_HEAD_DIM, (hd + 1) * MEM_HEAD_DIM)
        s = lax.dot_general(q[:, sl], mk[:, sl], NT_DIMS, preferred_element_type=F32)
        e = jnp.exp(s - jnp.max(s, axis=-1, keepdims=True))
        pr = e / jnp.sum(e, axis=-1, keepdims=True)
        outs.append(jnp.dot(pr.astype(BF16), mv[:, sl], preferred_element_type=F32))
    o = jnp.concatenate(outs, axis=1).astype(BF16)
    y_ref[...] = x + jnp.dot(o, wo_ref[...], preferred_element_type=F32)


def _mem_attn(x, g, wq, mk, mv, wo, *, tm, name):
    NB, T, D = x.shape
    n_mem = mk.shape[1]
    xs = pl.BlockSpec((None, tm, D), lambda b, i: (b, i, 0))
    ms = pl.BlockSpec((None, n_mem, D_MEM), lambda b, i: (b, 0, 0))
    est = 4 * tm * D * 4 + 4 * n_mem * D_MEM * 4 + 2 * D * D_MEM * 2 + 8 * tm * n_mem * 4
    return pl.pallas_call(
        _mem_attn_kernel,
        out_shape=jax.ShapeDtypeStruct((NB, T, D), F32),
        grid=(NB, T // tm),
        in_specs=[xs, pl.BlockSpec((1, D), lambda b, i: (0, 0)), _full(wq.shape), ms, ms, _full(wo.shape)],
        out_specs=xs,
        compiler_params=pltpu.CompilerParams(dimension_semantics=("arbitrary", "arbitrary"),
                                             vmem_limit_bytes=_vmem_limit(est)),
        name=name,
    )(x, g, wq, mk, mv, wo)


def _ffn_kernel(x_ref, g_ref, wg_ref, wu_ref, wd_ref, gf_ref, y_ref, *, final_norm):
    x = x_ref[...]
    xn = _rms(x, g_ref[...]).astype(BF16)
    d_ff = wg_ref.shape[1]
    acc = x
    for c in range(d_ff // V7X_MXU_DIM):
        sl = slice(c * V7X_MXU_DIM, (c + 1) * V7X_MXU_DIM)
        gate = jnp.dot(xn, wg_ref[:, sl], preferred_element_type=F32)
        up = jnp.dot(xn, wu_ref[:, sl], preferred_element_type=F32)
        acc = acc + jnp.dot((jax.nn.silu(gate) * up).astype(BF16), wd_ref[sl, :], preferred_element_type=F32)
    y_ref[...] = _rms(acc, gf_ref[...]) if final_norm else acc


def _ffn(x, g, wg, wu, wd, gf, *, final_norm, tm, name):
    R, D = x.shape
    row = pl.BlockSpec((tm, D), lambda i: (i, 0))
    vec = pl.BlockSpec((1, D), lambda i: (0, 0))
    est = 3 * D * wg.shape[1] * 2 + 4 * tm * D * 4 + 6 * tm * D * 4
    return pl.pallas_call(
        functools.partial(_ffn_kernel, final_norm=final_norm),
        out_shape=jax.ShapeDtypeStruct((R, D), F32),
        grid=(R // tm,),
        in_specs=[row, vec, _full(wg.shape), _full(wu.shape), _full(wd.shape), vec],
        out_specs=row,
        compiler_params=pltpu.CompilerParams(dimension_semantics=("arbitrary",), vmem_limit_bytes=_vmem_limit(est)),
        name=name,
    )(x, g, wg, wu, wd, gf)


def _block_diag_tiles(w):
    nb = w.shape[0]
    per = V7X_MXU_DIM // LRU_BLOCK
    eye = jnp.eye(per, dtype=w.dtype)
    w4 = w.reshape(nb // per, per, LRU_BLOCK, LRU_BLOCK)
    t = jnp.einsum('gakj,ab->gakbj', w4, eye)
    return t.reshape(nb // per, V7X_MXU_DIM, V7X_MXU_DIM).astype(BF16)


def _row(v):
    return v.reshape(1, -1).astype(F32)


def kernel(x_prompt, mem_prompt, x_sample, cache_k, cache_v, page_table, cache_mem_k, cache_mem_v, state_conv, state_h, norm_mix, w_in, conv_w, conv_b, w_rg, b_rg, w_ig, b_ig, lru_lambda, lam_q1, lam_k1, lam_q2, lam_k2, subln, w_br_lru, w_br_attn, w_out, norm_mem, norm_memkv, w_mem_q, w_mem_kv, w_mem_o, norm_ffn, w_gate_up, w_down, norm_final):
    depth = w_in.shape[0]
    B, S, D = x_prompt.shape
    DB, DS, _ = x_sample.shape
    n_mem = mem_prompt.shape[1]
    n_pool = cache_k.shape[1]
    n_pages = page_table.shape[1]
    d_ff = w_down.shape[1]
    HV = N_HEADS * V_DIM

    slopes_np = np.array([2.0 ** (-8.0 * (h + 1) / N_HEADS) for h in range(N_HEADS)], dtype=np.float32)
    slopes = jnp.asarray(slopes_np)
    slope_rows = jnp.asarray(np.repeat(slopes_np, 2 * DS).reshape(-1, 1))
    ck = cache_k.reshape(depth, n_pool, PAGE_SIZE * N_HEADS, V_DIM)
    cv = cache_v.reshape(depth, n_pool, PAGE_SIZE * N_HEADS, V_DIM)
    mem2d = mem_prompt.reshape(B * n_mem, D)
    gfin = _row(norm_final)

    xp = x_prompt.reshape(B * S, D)
    xs = x_sample.reshape(DB * DS, D)
    zc_p = jnp.zeros((B, CONV_W - 1, D_RNN), F32)
    zh_p = jnp.zeros((B, 1, D_RNN), F32)
    outs_p = {n: [] for n in ('k', 'v', 'mk', 'mv', 'c', 'h')}
    outs_s = {n: [] for n in ('k', 'v', 'c', 'h')}

    for layer in range(depth):
        lam_init = 0.8 - 0.6 * math.exp(-0.3 * layer)
        last = layer == depth - 1
        w_in_b = w_in[layer].astype(BF16)
        wrg = _block_diag_tiles(w_rg[layer])
        wig = _block_diag_tiles(w_ig[layer])
        lamp = jnp.stack([lam_q1[layer], lam_k1[layer], lam_q2[layer], lam_k2[layer]]).astype(F32)
        wbl, wba, wout = (w_br_lru[layer].astype(BF16), w_br_attn[layer].astype(BF16), w_out[layer].astype(BF16))
        wmq, wmo = w_mem_q[layer].astype(BF16), w_mem_o[layer].astype(BF16)
        wgu = w_gate_up[layer].astype(BF16)
        wg, wu, wd = wgu[:, :d_ff], wgu[:, d_ff:], w_down[layer].astype(BF16)
        lru_args = (conv_w[layer].astype(F32), _row(conv_b[layer]), wrg, _row(b_rg[layer]), wig, _row(b_ig[layer]),
                    _row(lru_lambda[layer]))

        def trunk(x2d, nb, t, mk, mv, conv_prev, h0, start_pos, attn_fn, tm, tm_mem, ns, L, tag):
            xl, gl, q, k, v, gtl, gta = _norm_proj(x2d, _row(norm_mix[layer]), w_in_b,
                                                   (F32, F32, BF16, F32, F32, F32, F32), min(tm, 256), f"in_proj_{tag}")
            hg, c_new, h_last = _lru(xl, gl, conv_prev, h0, *lru_args, ns=ns, L=L, start_pos=start_pos,
                                     name=f"lru_{tag}")
            o = attn_fn(q.reshape(nb, t, HV), k.reshape(nb, t, HV), v.reshape(nb, t, HV)).reshape(nb * t, HV)
            x2 = _mix(hg, o, gtl, gta, x2d, wbl, wba, wout, tm=tm, name=f"mix_{tag}")
            x3 = _mem_attn(x2.reshape(nb, t, D), _row(norm_mem[layer]), wmq, mk, mv, wmo, tm=tm_mem,
                           name=f"mem_attn_{tag}").reshape(nb * t, D)
            x4 = _ffn(x3, _row(norm_ffn[layer]), wg, wu, wd, gfin, final_norm=last, tm=tm, name=f"ffn_{tag}")
            return x4, c_new, h_last.reshape(nb, D_RNN), k.reshape(nb, t, N_HEADS, V_DIM), v.reshape(nb, t, N_HEADS, V_DIM)

        mk2, mv2 = _norm_proj(mem2d, _row(norm_memkv[layer]), w_mem_kv[layer].astype(BF16), (F32, F32), 256,
                              f"mem_kv_{layer}")
        mk_p = mk2.reshape(B, n_mem, D_MEM)
        mv_p = mv2.reshape(B, n_mem, D_MEM)
        attn_p = functools.partial(_attn_prompt, slopes=slopes, lamp=lamp,
                                   subln_col=subln[layer].reshape(V_DIM, 1).astype(F32),
                                   lam_init=lam_init, tq=512, hp=2, name=f"attn_prompt_{layer}")
        xp, c_new, h_last, k, v = trunk(xp, B, S, mk_p, mv_p, zc_p, zh_p, 0, attn_p, 512, 512, 1, 256, f"p{layer}")
        outs_p['k'].append(k); outs_p['v'].append(v)
        outs_p['mk'].append(mk_p.reshape(B, n_mem, MEM_HEADS, MEM_HEAD_DIM))
        outs_p['mv'].append(mv_p.reshape(B, n_mem, MEM_HEADS, MEM_HEAD_DIM))
        outs_p['c'].append(c_new); outs_p['h'].append(h_last)

        attn_s = functools.partial(_attn_sample, cache_k=ck, cache_v=cv, page_table=page_table,
                                   slope_rows=slope_rows, lamp=lamp, subln=_row(subln[layer]), layer=layer,
                                   lam_init=lam_init, pb=8, name=f"attn_sample_{layer}")
        xs, c_new, h_last, k, v = trunk(xs, DB, DS, cache_mem_k[layer].reshape(DB, n_mem, D_MEM),
                                        cache_mem_v[layer].reshape(DB, n_mem, D_MEM), state_conv[layer],
                                        state_h[layer].reshape(DB, 1, D_RNN), n_pages * PAGE_SIZE, attn_s,
                                        DB * DS, DS, DB, DS, f"s{layer}")
        outs_s['k'].append(k); outs_s['v'].append(v); outs_s['c'].append(c_new); outs_s['h'].append(h_last)

    st = jnp.stack
    return (xp.reshape(B, S, D), xs.reshape(DB, DS, D),
            st(outs_p['k']), st(outs_p['v']), st(outs_p['mk']), st(outs_p['mv']), st(outs_p['c']), st(outs_p['h']),
            st(outs_s['k']), st(outs_s['v']), st(outs_s['c']), st(outs_s['h']))
```

```python
import functools
import math

import numpy as np
import jax
import jax.numpy as jnp
from jax import lax
from jax.experimental import pallas as pl
from jax.experimental.pallas import tpu as pltpu

F32 = jnp.float32
BF16 = jnp.bfloat16

D_MODEL = 1024
D_RNN = D_MODEL
LRU_BLOCK = 64
CONV_W = 4
LRU_C = 8.0
N_HEADS = 8
HEAD_DIM = 64
V_DIM = 2 * HEAD_DIM
SCALE = HEAD_DIM ** -0.5
PAGE_SIZE = 128
MEM_HEADS = 4
MEM_HEAD_DIM = 64
D_MEM = MEM_HEADS * MEM_HEAD_DIM
EPS = 1e-6

V7X_VMEM_BYTES = 64 * 1024 * 1024
V7X_MXU_DIM = 256
LANES = 128
SUBLANES = 8

NT_DIMS = (((1,), (1,)), ((), ()))


def _vmem_limit(estimate_bytes):
    return int(min(V7X_VMEM_BYTES - 6 * 1024 * 1024, max(32 * 1024 * 1024, 2 * estimate_bytes)))


def _rms(x, g):
    return x * lax.rsqrt(jnp.mean(x * x, axis=-1, keepdims=True) + EPS) * g


def _full(shape):
    nd = len(shape)
    return pl.BlockSpec(shape, lambda *_: (0,) * nd, pipeline_mode=pl.Buffered(1))


def _norm_proj_kernel(x_ref, g_ref, w_ref, *outs):
    xn = _rms(x_ref[...], g_ref[...]).astype(BF16)
    wn = outs[0].shape[1]
    for idx, o_ref in enumerate(outs):
        y = jnp.dot(xn, w_ref[:, idx * wn:(idx + 1) * wn], preferred_element_type=F32)
        o_ref[...] = y.astype(o_ref.dtype)


def _norm_proj(x, g, w, out_dtypes, tm, name):
    R, D = x.shape
    n_out = len(out_dtypes)
    wn = w.shape[1] // n_out
    est = 2 * tm * D * 4 + D * w.shape[1] * 2 + n_out * 2 * tm * wn * 4 + tm * D * 2 + 2 * tm * wn * 4
    return pl.pallas_call(
        _norm_proj_kernel,
        out_shape=[jax.ShapeDtypeStruct((R, wn), dt) for dt in out_dtypes],
        grid=(R // tm,),
        in_specs=[pl.BlockSpec((tm, D), lambda i: (i, 0)),
                  pl.BlockSpec((1, D), lambda i: (0, 0)),
                  _full(w.shape)],
        out_specs=[pl.BlockSpec((tm, wn), lambda i: (i, 0))] * n_out,
        compiler_params=pltpu.CompilerParams(dimension_semantics=("arbitrary",),
                                             vmem_limit_bytes=_vmem_limit(est)),
        name=name,
    )(x, g, w)


def _lru_kernel(xl_ref, gl_ref, cprev_ref, h0_ref, cw_ref, cb_ref, wrg_ref, brg_ref, wig_ref, big_ref,
                lam_ref, hg_ref, cnew_ref, hlast_ref, cp_sc, hc_sc, *, ns, L, start_pos):
    t_blk = pl.program_id(1)
    rows = ns * L
    C = xl_ref.shape[-1]

    @pl.when(t_blk == 0)
    def _():
        cp_sc[...] = cprev_ref[...]
        hc_sc[...] = h0_ref[...]

    x = xl_ref[...]
    prev = cp_sc[...]

    t8 = lax.broadcasted_iota(jnp.int32, (ns, SUBLANES, C), 1)

    def shifted(k):
        r3 = pltpu.roll(x, k, axis=0).reshape(ns, L, C)
        head = r3[:, :SUBLANES, :]
        for t0 in range(k):
            fill = jnp.broadcast_to(prev[:, CONV_W - 1 - k + t0:CONV_W - k + t0, :], head.shape)
            head = jnp.where(t8 == t0, fill, head)
        r3 = head if L == SUBLANES else jnp.concatenate([head, r3[:, SUBLANES:, :]], axis=1)
        return r3.reshape(rows, C)

    cw = cw_ref[...]
    xc = cb_ref[...] + x * cw[3:4, :]
    for k in range(1, CONV_W):
        xc = xc + shifted(k) * cw[CONV_W - 1 - k:CONV_W - k, :]

    r_parts, i_parts = [], []
    for c in range(C // V7X_MXU_DIM):
        xb = xc[:, c * V7X_MXU_DIM:(c + 1) * V7X_MXU_DIM].astype(BF16)
        r_parts.append(jnp.dot(xb, wrg_ref[c], preferred_element_type=F32))
        i_parts.append(jnp.dot(xb, wig_ref[c], preferred_element_type=F32))
    r = jax.nn.sigmoid(jnp.concatenate(r_parts, axis=1) + brg_ref[...])
    ig = jax.nn.sigmoid(jnp.concatenate(i_parts, axis=1) + big_ref[...])

    nl = -lam_ref[...]
    softplus = jnp.maximum(nl, 0.0) + jnp.log1p(jnp.exp(-jnp.abs(nl)))
    log_a = (-LRU_C) * r * softplus
    a = jnp.exp(log_a)
    z = -jnp.tanh(log_a) * (a * a + 1.0)
    mult = jnp.where(z > 0.0, z * lax.rsqrt(z), 0.0)
    b = mult * (ig * xc)

    nt = L // SUBLANES
    a4 = a.reshape(ns, nt, SUBLANES, C)
    b4 = b.reshape(ns, nt, SUBLANES, C)
    first = t8 == 0
    a0, b0 = a4[:, 0], b4[:, 0]
    if start_pos == 0:
        reset = jnp.logical_and(first, t_blk == 0)
        gx = (ig * xc).reshape(ns, nt, SUBLANES, C)[:, 0]
        a0 = jnp.where(reset, 0.0, a0)
        b0 = jnp.where(reset, gx, b0)
    b0 = b0 + jnp.where(first, a0 * jnp.broadcast_to(hc_sc[...], a0.shape), 0.0)
    if nt > 1:
        a4 = jnp.concatenate([a0[:, None], a4[:, 1:]], axis=1)
        b4 = jnp.concatenate([b0[:, None], b4[:, 1:]], axis=1)
    else:
        a4, b4 = a0[:, None], b0[:, None]

    a3 = a4.reshape(ns * nt, SUBLANES, C)
    b3 = b4.reshape(ns * nt, SUBLANES, C)
    tin = lax.broadcasted_iota(jnp.int32, (ns * nt, SUBLANES, C), 1)
    for d in (1, 2, 4):
        valid = tin >= d
        b3 = b3 + jnp.where(valid, a3, 0.0) * pltpu.roll(b3, d, axis=1)
        a3 = jnp.where(valid, a3 * pltpu.roll(a3, d, axis=1), a3)
    if nt == 1:
        h = b3.reshape(rows, C)
    else:
        a4 = a3.reshape(ns, nt, SUBLANES, C)
        b4 = b3.reshape(ns, nt, SUBLANES, C)
        tiles = [b4[:, 0]]
        for kt in range(1, nt):
            carry = jnp.broadcast_to(tiles[-1][:, SUBLANES - 1:SUBLANES, :], (ns, SUBLANES, C))
            tiles.append(a4[:, kt] * carry + b4[:, kt])
        h = jnp.stack(tiles, axis=1).reshape(rows, C)

    hg_ref[...] = (h * jax.nn.gelu(gl_ref[...])).astype(hg_ref.dtype)
    h3 = h.reshape(ns, L, C)
    x3 = x.reshape(ns, L, C)
    hl = h3[:, L - 1:L, :]
    tail = x3[:, L - (CONV_W - 1):L, :]
    hc_sc[...] = hl
    cp_sc[...] = tail
    hlast_ref[...] = hl
    cnew_ref[...] = tail


def _lru(xl, gl, conv_prev, h0, cw, cb, wrg, brg, wig, big, lam, *, ns, L, start_pos, name):
    NS = conv_prev.shape[0]
    R, C = xl.shape
    T = R // NS
    nt = T // L
    rows = ns * L
    est = 2 * 2 * rows * C * 4 + 2 * rows * C * 2 + 12 * rows * C * 4 + 2 * 4 * 256 * 256 * 2
    row_spec = pl.BlockSpec((rows, C), lambda s, t: (s * nt + t, 0))
    vec = pl.BlockSpec((1, C), lambda s, t: (0, 0))
    return pl.pallas_call(
        functools.partial(_lru_kernel, ns=ns, L=L, start_pos=start_pos),
        out_shape=[jax.ShapeDtypeStruct((R, C), BF16),
                   jax.ShapeDtypeStruct((NS, CONV_W - 1, C), F32),
                   jax.ShapeDtypeStruct((NS, 1, C), F32)],
        grid=(NS // ns, nt),
        in_specs=[row_spec, row_spec,
                  pl.BlockSpec((ns, CONV_W - 1, C), lambda s, t: (s, 0, 0)),
                  pl.BlockSpec((ns, 1, C), lambda s, t: (s, 0, 0)),
                  pl.BlockSpec((CONV_W, C), lambda s, t: (0, 0)), vec,
                  pl.BlockSpec(wrg.shape, lambda s, t: (0, 0, 0)), vec,
                  pl.BlockSpec(wig.shape, lambda s, t: (0, 0, 0)), vec, vec],
        out_specs=[row_spec,
                   pl.BlockSpec((ns, CONV_W - 1, C), lambda s, t: (s, 0, 0)),
                   pl.BlockSpec((ns, 1, C), lambda s, t: (s, 0, 0))],
        scratch_shapes=[pltpu.VMEM((ns, CONV_W - 1, C), F32), pltpu.VMEM((ns, 1, C), F32)],
        compiler_params=pltpu.CompilerParams(dimension_semantics=("arbitrary", "arbitrary"),
                                             vmem_limit_bytes=_vmem_limit(est)),
        name=name,
    )(xl, gl, conv_prev, h0, cw, cb, wrg, brg, wig, big, lam)


def _diff_lambda(lamp, lam_init):
    l1 = jnp.sum(lamp[0:1, :] * lamp[1:2, :], axis=-1, keepdims=True)
    l2 = jnp.sum(lamp[2:3, :] * lamp[3:4, :], axis=-1, keepdims=True)
    return jnp.exp(l1) - jnp.exp(l2) + lam_init


def _stack_maps(q):
    lane = lax.broadcasted_iota(jnp.int32, q.shape, 1)
    return jnp.concatenate([jnp.where(lane < HEAD_DIM, q, 0.0), jnp.where(lane >= HEAD_DIM, q, 0.0)], axis=0)


POS_DIGITS = (256, 16, 1)
ATTN_COL_CHUNK = V7X_MXU_DIM
VT_ROWS = V_DIM + 16


def _attn_prompt_kernel(slopes_ref, q_ref, k_ref, v_ref, lamp_ref, subln_ref, o_ref,
                        qa_sc, ka_sc, vt_sc, m_sc, acc_sc, *, tq, hp, lam_init):
    i = pl.program_id(2)
    tk = tq
    S = k_ref.shape[0]
    dk = 2 * HEAD_DIM

    @pl.when(i == 0)
    def _():
        pos = lax.broadcasted_iota(jnp.int32, (S, dk), 0)
        lane = lax.broadcasted_iota(jnp.int32, (S, dk), 1)
        digit = jnp.where(lane == 0, pos >> 8, jnp.where(lane == 1, (pos >> 4) & 15, jnp.where(lane == 2, pos & 15, 0)))
        digit = digit.astype(F32).astype(BF16)
        for g in range(hp):
            ka_sc[g, :, :dk] = k_ref[:, g * dk:(g + 1) * dk].astype(BF16)
            ka_sc[g, :, dk:] = digit
            vt = v_ref[:, g * V_DIM:(g + 1) * V_DIM].T
            for jb in range(S // tk):
                vt_sc[g, jb, :V_DIM, :] = vt[:, jb * tk:(jb + 1) * tk].astype(BF16)
                vt_sc[g, jb, V_DIM:, :] = jnp.ones((VT_ROWS - V_DIM, tk), BF16)

    lane = lax.broadcasted_iota(jnp.int32, (2 * tq, dk), 1)
    for g in range(hp):
        slope = slopes_ref[pl.program_id(1) * hp + g]
        qa_sc[g, :, :dk] = _stack_maps(q_ref[:, g * dk:(g + 1) * dk].astype(F32) * SCALE).astype(BF16)
        posw = jnp.where(lane == 0, POS_DIGITS[0] * slope,
                         jnp.where(lane == 1, POS_DIGITS[1] * slope, jnp.where(lane == 2, POS_DIGITS[2] * slope, 0.0)))
        qa_sc[g, :, dk:] = posw.astype(BF16)
    m_sc[...] = jnp.full(m_sc.shape, -jnp.inf, F32)
    acc_sc[...] = jnp.zeros(acc_sc.shape, F32)

    def kv_block(j, diag):
        start = pl.multiple_of(j * tk, tk)
        chains = [(g, c0) for g in range(hp) for c0 in range(0, 2 * tq, ATTN_COL_CHUNK)]
        probs = []
        for g, c0 in chains:
            cols = slice(c0, c0 + ATTN_COL_CHUNK)
            s = lax.dot_general(ka_sc[g, pl.ds(start, tk), :], qa_sc[g, cols, :], NT_DIMS,
                                preferred_element_type=F32)
            if diag:
                qcol = (lax.broadcasted_iota(jnp.int32, s.shape, 1) + c0) & (tq - 1)
                s = jnp.where(qcol >= lax.broadcasted_iota(jnp.int32, s.shape, 0), s, -jnp.inf)
            m_old = m_sc[g, :, cols]
            m_new = jnp.maximum(m_old, jnp.max(s, axis=0, keepdims=True))
            m_sc[g, :, cols] = m_new
            probs.append((jnp.exp(s - m_new).astype(BF16), jnp.exp(m_old - m_new)))
        for (g, c0), (p, corr) in zip(chains, probs):
            cols = slice(c0, c0 + ATTN_COL_CHUNK)
            pv = jnp.dot(vt_sc[g, j], p, preferred_element_type=F32)
            acc_sc[g, :, cols] = acc_sc[g, :, cols] * corr + pv

    def body(j, carry):
        kv_block(j, False)
        return carry

    lax.fori_loop(0, i, body, 0)
    kv_block(i, True)

    lam = _diff_lambda(lamp_ref[...], lam_init)
    for g in range(hp):
        acc = acc_sc[g]
        o = acc[:V_DIM, :] / acc[V_DIM:V_DIM + 1, :]
        od = o[:, :tq] - lam * o[:, tq:]
        y = od * lax.rsqrt(jnp.mean(od * od, axis=0, keepdims=True) + EPS) * subln_ref[...] * (1.0 - lam_init)
        o_ref[:, g * V_DIM:(g + 1) * V_DIM] = y.T.astype(o_ref.dtype)


def _attn_prompt(q, k, v, slopes, lamp, subln_col, *, lam_init, tq, hp, name):
    B, S, _ = q.shape
    assert S <= 256 * POS_DIGITS[0] and (2 * tq) % ATTN_COL_CHUNK == 0 and S % tq == 0 and N_HEADS % hp == 0
    est = (2 * 2 * S * hp * 128 * 4 + hp * (S * 256 * 2 + S * VT_ROWS * 2 + 2 * tq * 256 * 2 + VT_ROWS * 2 * tq * 4)
           + 2 * hp * tq * 2 * tq * 4 + S * 128 * 4)
    return pl.pallas_call(
        functools.partial(_attn_prompt_kernel, tq=tq, hp=hp, lam_init=lam_init),
        out_shape=jax.ShapeDtypeStruct((B, S, N_HEADS * V_DIM), BF16),
        grid=(B, N_HEADS // hp, S // tq),
        in_specs=[pl.BlockSpec(memory_space=pltpu.SMEM),
                  pl.BlockSpec((None, tq, hp * 2 * HEAD_DIM), lambda b, h, i: (b, i, h)),
                  pl.BlockSpec((None, S, hp * 2 * HEAD_DIM), lambda b, h, i: (b, 0, h)),
                  pl.BlockSpec((None, S, hp * V_DIM), lambda b, h, i: (b, 0, h)),
                  pl.BlockSpec((4, HEAD_DIM), lambda b, h, i: (0, 0)),
                  pl.BlockSpec((V_DIM, 1), lambda b, h, i: (0, 0))],
        out_specs=pl.BlockSpec((None, tq, hp * V_DIM), lambda b, h, i: (b, i, h)),
        scratch_shapes=[pltpu.VMEM((hp, 2 * tq, 4 * HEAD_DIM), BF16),
                        pltpu.VMEM((hp, S, 4 * HEAD_DIM), BF16), pltpu.VMEM((hp, S // tq, VT_ROWS, tq), BF16),
                        pltpu.VMEM((hp, 1, 2 * tq), F32), pltpu.VMEM((hp, VT_ROWS, 2 * tq), F32)],
        compiler_params=pltpu.CompilerParams(dimension_semantics=("arbitrary", "arbitrary", "arbitrary"),
                                             vmem_limit_bytes=_vmem_limit(est)),
        name=name,
    )(slopes, q, k, v, lamp, subln_col)


def _attn_sample_kernel(pt_ref, q_ref, k_ref, v_ref, slope_ref, lamp_ref, subln_ref, *rest,
                        pb, ds, past_len, lam_init):
    kpages, vpages = rest[:pb], rest[pb:2 * pb]
    o_ref, qs_sc, ko_sc, vo_sc, m_sc, l_sc, acc_sc = rest[2 * pb:]
    p = pl.program_id(1)
    hh = N_HEADS // 2
    nq = 2 * ds
    nrow = hh * nq
    ncol = 2 * PAGE_SIZE
    row = lax.broadcasted_iota(jnp.int32, (nrow, ncol), 0)
    col = lax.broadcasted_iota(jnp.int32, (nrow, ncol), 1)
    even = (col & 1) == 0
    E, O = slice(0, nrow), slice(nrow, 2 * nrow)
    bcast = lambda a, shape=(nrow, ncol): jnp.broadcast_to(a, shape)
    slope = jnp.where(even, bcast(slope_ref[E, :]), bcast(slope_ref[O, :]))
    dq = (row & (ds - 1)) - (col >> 1)
    nsd = (-slope) * dq.astype(F32)

    def pair_rows(ref, a):
        return ref[pl.ds(a, ncol, stride=hh), :].astype(BF16)

    def update(krefs, vrefs, biases):
        n = len(krefs)
        ev = (lax.broadcasted_iota(jnp.int32, (nrow, n * ncol), 1) & 1) == 0
        ev_head = (lax.broadcasted_iota(jnp.int32, (nq, n * ncol), 1) & 1) == 0
        s_pairs = []
        for a in range(hh):
            ka = jnp.concatenate([pair_rows(r, a) for r in krefs], axis=0)
            s2 = lax.dot_general(qs_sc[a * 2 * nq:(a + 1) * 2 * nq, :], ka, NT_DIMS, preferred_element_type=F32)
            s_pairs.append(jnp.where(ev_head, s2[:nq], s2[nq:]))
        s = jnp.concatenate(s_pairs, axis=0) + jnp.concatenate(biases, axis=1)
        m_old_e, m_old_o = m_sc[E, :], m_sc[O, :]
        m_e = jnp.maximum(m_old_e, jnp.max(jnp.where(ev, s, -jnp.inf), axis=1, keepdims=True))
        m_o = jnp.maximum(m_old_o, jnp.max(jnp.where(ev, -jnp.inf, s), axis=1, keepdims=True))
        pr = jnp.exp(s - jnp.where(ev, bcast(m_e, s.shape), bcast(m_o, s.shape)))
        pe = jnp.where(ev, pr, 0.0)
        po = jnp.where(ev, 0.0, pr)
        corr_e = jnp.exp(m_old_e - m_e)
        corr_o = jnp.exp(m_old_o - m_o)
        l_sc[E, :] = l_sc[E, :] * corr_e + jnp.sum(pe, axis=1, keepdims=True)
        l_sc[O, :] = l_sc[O, :] * corr_o + jnp.sum(po, axis=1, keepdims=True)
        for a in range(hh):
            va = jnp.concatenate([pair_rows(r, a) for r in vrefs], axis=0)
            rs = slice(a * nq, (a + 1) * nq)
            p2 = jnp.concatenate([pe[rs], po[rs]], axis=0).astype(BF16)
            pv = jnp.dot(p2, va, preferred_element_type=F32)
            acc_sc[a, 0] = acc_sc[a, 0] * corr_e[rs] + pv[:nq]
            acc_sc[a, 1] = acc_sc[a, 1] * corr_o[rs] + pv[nq:]
        m_sc[E, :] = m_e
        m_sc[O, :] = m_o

    @pl.when(p == 0)
    def _():
        q = q_ref[...].astype(F32) * SCALE
        k_own, v_own = k_ref[...], v_ref[...]
        ko_sc[...] = jnp.zeros(ko_sc.shape, F32)
        vo_sc[...] = jnp.zeros(vo_sc.shape, F32)
        for hd in range(N_HEADS):
            cols = slice(hd * V_DIM, (hd + 1) * V_DIM)
            r0 = (hd % hh) * 2 * nq + (hd // hh) * nq
            qs_sc[r0:r0 + nq, :] = _stack_maps(q[:, cols]).astype(BF16)
            ko_sc[pl.ds(hd, ds, stride=N_HEADS), :] = k_own[:, cols]
            vo_sc[pl.ds(hd, ds, stride=N_HEADS), :] = v_own[:, cols]
        m_sc[...] = jnp.full(m_sc.shape, -jnp.inf, F32)
        l_sc[...] = jnp.zeros(l_sc.shape, F32)
        acc_sc[...] = jnp.zeros(acc_sc.shape, F32)
        update([ko_sc], [vo_sc], [jnp.where(dq >= 0, nsd, -jnp.inf)])

    dist0 = [(past_len - (p * pb + ip) * PAGE_SIZE).astype(F32) for ip in range(pb)]
    update(kpages, vpages, [nsd - slope * d0 for d0 in dist0])

    @pl.when(p == pl.num_programs(1) - 1)
    def _():
        lam = _diff_lambda(lamp_ref[...], lam_init)
        for hd in range(N_HEADS):
            a, member = hd % hh, hd // hh
            o = acc_sc[a, member] / l_sc[member * nrow + a * nq:member * nrow + (a + 1) * nq, :]
            od = o[:ds, :] - lam * o[ds:, :]
            o_ref[:, hd * V_DIM:(hd + 1) * V_DIM] = (_rms(od, subln_ref[...]) * (1.0 - lam_init)).astype(o_ref.dtype)


def _attn_sample(q, k, v, cache_k, cache_v, page_table, slope_pairs, lamp, subln, *, layer, lam_init, pb, name):
    DB, DS, HV = q.shape
    n_pages = page_table.shape[1]
    assert n_pages % pb == 0 and N_HEADS % 2 == 0
    nrow = (N_HEADS // 2) * 2 * DS
    prow = PAGE_SIZE * N_HEADS
    tok_spec = pl.BlockSpec((None, DS, HV), lambda b, p, pt: (b, 0, 0))

    def page_spec(ip):
        return pl.BlockSpec((None, None, prow, V_DIM), lambda b, p, pt: (layer, pt[b, p * pb + ip], 0, 0))

    est = 2 * 2 * pb * PAGE_SIZE * HV * 4 + 2 * PAGE_SIZE * HV * 4 + 16 * nrow * pb * 2 * PAGE_SIZE * 4
    grid_spec = pltpu.PrefetchScalarGridSpec(
        num_scalar_prefetch=1,
        grid=(DB, n_pages // pb),
        in_specs=[tok_spec, tok_spec, tok_spec,
                  pl.BlockSpec((2 * nrow, 1), lambda b, p, pt: (0, 0)),
                  pl.BlockSpec((4, HEAD_DIM), lambda b, p, pt: (0, 0)),
                  pl.BlockSpec((1, V_DIM), lambda b, p, pt: (0, 0))]
                 + [page_spec(ip) for ip in range(pb)] * 2,
        out_specs=tok_spec,
        scratch_shapes=[pltpu.VMEM((2 * nrow, V_DIM), BF16),
                        pltpu.VMEM((prow, V_DIM), F32), pltpu.VMEM((prow, V_DIM), F32),
                        pltpu.VMEM((2 * nrow, 1), F32), pltpu.VMEM((2 * nrow, 1), F32),
                        pltpu.VMEM((N_HEADS // 2, 2, 2 * DS, V_DIM), F32)])
    return pl.pallas_call(
        functools.partial(_attn_sample_kernel, pb=pb, ds=DS, past_len=n_pages * PAGE_SIZE, lam_init=lam_init),
        out_shape=jax.ShapeDtypeStruct((DB, DS, HV), F32),
        grid_spec=grid_spec,
        compiler_params=pltpu.CompilerParams(dimension_semantics=("arbitrary", "arbitrary"),
                                             vmem_limit_bytes=_vmem_limit(est)),
        name=name,
    )(page_table, q, k, v, slope_pairs, lamp, subln, *([cache_k] * pb), *([cache_v] * pb))


def _mix_kernel(hg_ref, o_ref, gl_ref, ga_ref, x_ref, wbl_ref, wba_ref, wout_ref, y_ref):
    bl = jnp.dot(hg_ref[...].astype(BF16), wbl_ref[...], preferred_element_type=F32)
    ba = jnp.dot(o_ref[...].astype(BF16), wba_ref[...], preferred_element_type=F32)
    mixed = jax.nn.sigmoid(gl_ref[...]) * bl + jax.nn.sigmoid(ga_ref[...]) * ba
    y_ref[...] = x_ref[...] + jnp.dot(mixed.astype(BF16), wout_ref[...], preferred_element_type=F32)


def _mix(hg, o, gl, ga, x, wbl, wba, wout, *, tm, name):
    R, D = x.shape
    row = lambda a: pl.BlockSpec((tm, a.shape[1]), lambda i: (i, 0))
    est = 3 * D * D * 2 + 2 * tm * D * (2 + 4 + 4 + 4 + 4 + 4) + 4 * tm * D * 4
    return pl.pallas_call(
        _mix_kernel,
        out_shape=jax.ShapeDtypeStruct((R, D), F32),
        grid=(R // tm,),
        in_specs=[row(hg), row(o), row(gl), row(ga), row(x), _full(wbl.shape), _full(wba.shape), _full(wout.shape)],
        out_specs=row(x),
        compiler_params=pltpu.CompilerParams(dimension_semantics=("arbitrary",), vmem_limit_bytes=_vmem_limit(est)),
        name=name,
    )(hg, o, gl, ga, x, wbl, wba, wout)


def _mem_attn_kernel(x_ref, g_ref, wq_ref, mk_ref, mv_ref, wo_ref, y_ref):
    x = x_ref[...]
    xn = _rms(x, g_ref[...]).astype(BF16)
    q = (jnp.dot(xn, wq_ref[...], preferred_element_type=F32) * (MEM_HEAD_DIM ** -0.5)).astype(BF16)
    mk = mk_ref[...].astype(BF16)
    mv = mv_ref[...].astype(BF16)
    outs = []
    for hd in range(MEM_HEADS):
        sl = slice(hd * MEM_HEAD_DIM, (hd + 1) * MEM_HEAD_DIM)
        s = lax.dot_general(q[:, sl], mk[:, sl], NT_DIMS, preferred_element_type=F32)
        e = jnp.exp(s - jnp.max(s, axis=-1, keepdims=True))
        pr = e / jnp.sum(e, axis=-1, keepdims=True)
        outs.append(jnp.dot(pr.astype(BF16), mv[:, sl], preferred_element_type=F32))
    o = jnp.concatenate(outs, axis=1).astype(BF16)
    y_ref[...] = x + jnp.dot(o, wo_ref[...], preferred_element_type=F32)


def _mem_attn(x, g, wq, mk, mv, wo, *, tm, name):
    NB, T, D = x.shape
    n_mem = mk.shape[1]
    xs = pl.BlockSpec((None, tm, D), lambda b, i: (b, i, 0))
    ms = pl.BlockSpec((None, n_mem, D_MEM), lambda b, i: (b, 0, 0))
    est = 4 * tm * D * 4 + 4 * n_mem * D_MEM * 4 + 2 * D * D_MEM * 2 + 8 * tm * n_mem * 4
    return pl.pallas_call(
        _mem_attn_kernel,
        out_shape=jax.ShapeDtypeStruct((NB, T, D), F32),
        grid=(NB, T // tm),
        in_specs=[xs, pl.BlockSpec((1, D), lambda b, i: (0, 0)), _full(wq.shape), ms, ms, _full(wo.shape)],
        out_specs=xs,
        compiler_params=pltpu.CompilerParams(dimension_semantics=("arbitrary", "arbitrary"),
                                             vmem_limit_bytes=_vmem_limit(est)),
        name=name,
    )(x, g, wq, mk, mv, wo)


def _ffn_kernel(x_ref, g_ref, wg_ref, wu_ref, wd_ref, gf_ref, y_ref, *, final_norm):
    x = x_ref[...]
    xn = _rms(x, g_ref[...]).astype(BF16)
    d_ff = wg_ref.shape[1]
    acc = x
    for c in range(d_ff // V7X_MXU_DIM):
        sl = slice(c * V7X_MXU_DIM, (c + 1) * V7X_MXU_DIM)
        gate = jnp.dot(xn, wg_ref[:, sl], preferred_element_type=F32)
        up = jnp.dot(xn, wu_ref[:, sl], preferred_element_type=F32)
        acc = acc + jnp.dot((jax.nn.silu(gate) * up).astype(BF16), wd_ref[sl, :], preferred_element_type=F32)
    y_ref[...] = _rms(acc, gf_ref[...]) if final_norm else acc


def _ffn(x, g, wg, wu, wd, gf, *, final_norm, tm, name):
    R, D = x.shape
    row = pl.BlockSpec((tm, D), lambda i: (i, 0))
    vec = pl.BlockSpec((1, D), lambda i: (0, 0))
    est = 3 * D * wg.shape[1] * 2 + 4 * tm * D * 4 + 6 * tm * D * 4
    return pl.pallas_call(
        functools.partial(_ffn_kernel, final_norm=final_norm),
        out_shape=jax.ShapeDtypeStruct((R, D), F32),
        grid=(R // tm,),
        in_specs=[row, vec, _full(wg.shape), _full(wu.shape), _full(wd.shape), vec],
        out_specs=row,
        compiler_params=pltpu.CompilerParams(dimension_semantics=("arbitrary",), vmem_limit_bytes=_vmem_limit(est)),
        name=name,
    )(x, g, wg, wu, wd, gf)


def _block_diag_tiles(w):
    nb = w.shape[0]
    per = V7X_MXU_DIM // LRU_BLOCK
    eye = jnp.eye(per, dtype=w.dtype)
    w4 = w.reshape(nb // per, per, LRU_BLOCK, LRU_BLOCK)
    t = jnp.einsum('gakj,ab->gakbj', w4, eye)
    return t.reshape(nb // per, V7X_MXU_DIM, V7X_MXU_DIM).astype(BF16)


def _row(v):
    return v.reshape(1, -1).astype(F32)


def kernel(x_prompt, mem_prompt, x_sample, cache_k, cache_v, page_table, cache_mem_k, cache_mem_v, state_conv, state_h, norm_mix, w_in, conv_w, conv_b, w_rg, b_rg, w_ig, b_ig, lru_lambda, lam_q1, lam_k1, lam_q2, lam_k2, subln, w_br_lru, w_br_attn, w_out, norm_mem, norm_memkv, w_mem_q, w_mem_kv, w_mem_o, norm_ffn, w_gate_up, w_down, norm_final):
    depth = w_in.shape[0]
    B, S, D = x_prompt.shape
    DB, DS, _ = x_sample.shape
    n_mem = mem_prompt.shape[1]
    n_pool = cache_k.shape[1]
    n_pages = page_table.shape[1]
    d_ff = w_down.shape[1]
    HV = N_HEADS * V_DIM

    slopes_np = np.array([2.0 ** (-8.0 * (h + 1) / N_HEADS) for h in range(N_HEADS)], dtype=np.float32)
    slopes = jnp.asarray(slopes_np)
    slope_pairs = jnp.asarray(np.repeat(slopes_np, 2 * DS).reshape(-1, 1))
    ck = cache_k.reshape(depth, n_pool, PAGE_SIZE * N_HEADS, V_DIM)
    cv = cache_v.reshape(depth, n_pool, PAGE_SIZE * N_HEADS, V_DIM)
    mem2d = mem_prompt.reshape(B * n_mem, D)
    gfin = _row(norm_final)

    xp = x_prompt.reshape(B * S, D)
    xs = x_sample.reshape(DB * DS, D)
    zc_p = jnp.zeros((B, CONV_W - 1, D_RNN), F32)
    zh_p = jnp.zeros((B, 1, D_RNN), F32)
    outs_p = {n: [] for n in ('k', 'v', 'mk', 'mv', 'c', 'h')}
    outs_s = {n: [] for n in ('k', 'v', 'c', 'h')}

    for layer in range(depth):
        lam_init = 0.8 - 0.6 * math.exp(-0.3 * layer)
        last = layer == depth - 1
        w_in_b = w_in[layer].astype(BF16)
        wrg = _block_diag_tiles(w_rg[layer])
        wig = _block_diag_tiles(w_ig[layer])
        lamp = jnp.stack([lam_q1[layer], lam_k1[layer], lam_q2[layer], lam_k2[layer]]).astype(F32)
        wbl, wba, wout = (w_br_lru[layer].astype(BF16), w_br_attn[layer].astype(BF16), w_out[layer].astype(BF16))
        wmq, wmo = w_mem_q[layer].astype(BF16), w_mem_o[layer].astype(BF16)
        wgu = w_gate_up[layer].astype(BF16)
        wg, wu, wd = wgu[:, :d_ff], wgu[:, d_ff:], w_down[layer].astype(BF16)
        lru_args = (conv_w[layer].astype(F32), _row(conv_b[layer]), wrg, _row(b_rg[layer]), wig, _row(b_ig[layer]),
                    _row(lru_lambda[layer]))

        def trunk(x2d, nb, t, mk, mv, conv_prev, h0, start_pos, attn_fn, tm, tm_mem, ns, L, tag):
            xl, gl, q, k, v, gtl, gta = _norm_proj(x2d, _row(norm_mix[layer]), w_in_b,
                                                   (F32, F32, BF16, F32, F32, F32, F32), min(tm, 256), f"in_proj_{tag}")
            hg, c_new, h_last = _lru(xl, gl, conv_prev, h0, *lru_args, ns=ns, L=L, start_pos=start_pos,
                                     name=f"lru_{tag}")
            o = attn_fn(q.reshape(nb, t, HV), k.reshape(nb, t, HV), v.reshape(nb, t, HV)).reshape(nb * t, HV)
            x2 = _mix(hg, o, gtl, gta, x2d, wbl, wba, wout, tm=tm, name=f"mix_{tag}")
            x3 = _mem_attn(x2.reshape(nb, t, D), _row(norm_mem[layer]), wmq, mk, mv, wmo, tm=tm_mem,
                           name=f"mem_attn_{tag}").reshape(nb * t, D)
            x4 = _ffn(x3, _row(norm_ffn[layer]), wg, wu, wd, gfin, final_norm=last, tm=tm, name=f"ffn_{tag}")
            return x4, c_new, h_last.reshape(nb, D_RNN), k.reshape(nb, t, N_HEADS, V_DIM), v.reshape(nb, t, N_HEADS, V_DIM)

        mk2, mv2 = _norm_proj(mem2d, _row(norm_memkv[layer]), w_mem_kv[layer].astype(BF16), (F32, F32), 256,
                              f"mem_kv_{layer}")
        mk_p = mk2.reshape(B, n_mem, D_MEM)
        mv_p = mv2.reshape(B, n_mem, D_MEM)
        attn_p = functools.partial(_attn_prompt, slopes=slopes, lamp=lamp,
                                   subln_col=subln[layer].reshape(V_DIM, 1).astype(F32),
                                   lam_init=lam_init, tq=512, hp=2, name=f"attn_prompt_{layer}")
        xp, c_new, h_last, k, v = trunk(xp, B, S, mk_p, mv_p, zc_p, zh_p, 0, attn_p, 512, 512, 1, 256, f"p{layer}")
        outs_p['k'].append(k); outs_p['v'].append(v)
        outs_p['mk'].append(mk_p.reshape(B, n_mem, MEM_HEADS, MEM_HEAD_DIM))
        outs_p['mv'].append(mv_p.reshape(B, n_mem, MEM_HEADS, MEM_HEAD_DIM))
        outs_p['c'].append(c_new); outs_p['h'].append(h_last)

        attn_s = functools.partial(_attn_sample, cache_k=ck, cache_v=cv, page_table=page_table,
                                   slope_pairs=slope_pairs, lamp=lamp, subln=_row(subln[layer]), layer=layer,
                                   lam_init=lam_init, pb=8, name=f"attn_sample_{layer}")
        xs, c_new, h_last, k, v = trunk(xs, DB, DS, cache_mem_k[layer].reshape(DB, n_mem, D_MEM),
                                        cache_mem_v[layer].reshape(DB, n_mem, D_MEM), state_conv[layer],
                                        state_h[layer].reshape(DB, 1, D_RNN), n_pages * PAGE_SIZE, attn_s,
                                        DB * DS, DS, DB, DS, f"s{layer}")
        outs_s['k'].append(k); outs_s['v'].append(v); outs_s['c'].append(c_new); outs_s['h'].append(h_last)

    st = jnp.stack
    return (xp.reshape(B, S, D), xs.reshape(DB, DS, D),
            st(outs_p['k']), st(outs_p['v']), st(outs_p['mk']), st(outs_p['mv']), st(outs_p['c']), st(outs_p['h']),
            st(outs_s['k']), st(outs_s['v']), st(outs_s['c']), st(outs_s['h']))
```

```python
import functools
import math

import numpy as np
import jax
import jax.numpy as jnp
from jax import lax
from jax.experimental import pallas as pl
from jax.experimental.pallas import tpu as pltpu

F32 = jnp.float32
BF16 = jnp.bfloat16

D_MODEL = 1024
D_RNN = D_MODEL
LRU_BLOCK = 64
CONV_W = 4
LRU_C = 8.0
N_HEADS = 8
HEAD_DIM = 64
V_DIM = 2 * HEAD_DIM
SCALE = HEAD_DIM ** -0.5
PAGE_SIZE = 128
MEM_HEADS = 4
MEM_HEAD_DIM = 64
D_MEM = MEM_HEADS * MEM_HEAD_DIM
EPS = 1e-6

V7X_VMEM_BYTES = 64 * 1024 * 1024
V7X_MXU_DIM = 256
LANES = 128
SUBLANES = 8

NT_DIMS = (((1,), (1,)), ((), ()))


def _vmem_limit(estimate_bytes):
    return int(min(V7X_VMEM_BYTES - 6 * 1024 * 1024, max(32 * 1024 * 1024, 2 * estimate_bytes)))


def _rms(x, g):
    return x * lax.rsqrt(jnp.mean(x * x, axis=-1, keepdims=True) + EPS) * g


def _full(shape):
    nd = len(shape)
    return pl.BlockSpec(shape, lambda *_: (0,) * nd, pipeline_mode=pl.Buffered(1))


def _norm_proj_kernel(x_ref, g_ref, w_ref, *rest, stacked):
    prevs, outs = rest[:len(stacked)], rest[len(stacked):]
    xn = _rms(x_ref[...], g_ref[...]).astype(BF16)
    wn = outs[0].shape[-1]
    for idx, o_ref in enumerate(outs):
        y = jnp.dot(xn, w_ref[:, idx * wn:(idx + 1) * wn], preferred_element_type=F32).astype(o_ref.dtype)
        if idx in stacked:
            prev_ref = prevs[stacked.index(idx)]
            n_prev = prev_ref.shape[0]
            o_ref[0:n_prev] = prev_ref[...]
            o_ref[n_prev] = y
        else:
            o_ref[...] = y


def _norm_proj(x, g, w, out_dtypes, tm, name, stack_onto=None):
    R, D = x.shape
    n_out = len(out_dtypes)
    wn = w.shape[1] // n_out
    stack_onto = stack_onto or {}
    stacked = tuple(sorted(stack_onto))
    depth_of = lambda idx: stack_onto[idx].shape[0] + 1 if idx in stack_onto else 0
    est = (2 * tm * D * 4 + D * w.shape[1] * 2 + tm * D * 2 + 2 * tm * wn * 4
           + sum(2 * tm * wn * 4 * max(1, 2 * depth_of(idx) - 1) for idx in range(n_out)))
    out_shape, out_specs = [], []
    for idx, dt in enumerate(out_dtypes):
        if idx in stack_onto:
            out_shape.append(jax.ShapeDtypeStruct((depth_of(idx), R, wn), dt))
            out_specs.append(pl.BlockSpec((depth_of(idx), tm, wn), lambda i: (0, i, 0)))
        else:
            out_shape.append(jax.ShapeDtypeStruct((R, wn), dt))
            out_specs.append(pl.BlockSpec((tm, wn), lambda i: (i, 0)))
    prev_specs = [pl.BlockSpec((stack_onto[idx].shape[0], tm, wn), lambda i: (0, i, 0)) for idx in stacked]
    return pl.pallas_call(
        functools.partial(_norm_proj_kernel, stacked=stacked),
        out_shape=out_shape,
        grid=(R // tm,),
        in_specs=[pl.BlockSpec((tm, D), lambda i: (i, 0)),
                  pl.BlockSpec((1, D), lambda i: (0, 0)),
                  _full(w.shape)] + prev_specs,
        out_specs=out_specs,
        compiler_params=pltpu.CompilerParams(dimension_semantics=("arbitrary",),
                                             vmem_limit_bytes=_vmem_limit(est)),
        name=name,
    )(x, g, w, *[stack_onto[idx] for idx in stacked])


def _lru_kernel(xl_ref, gl_ref, cprev_ref, h0_ref, cw_ref, cb_ref, wrg_ref, brg_ref, wig_ref, big_ref,
                lam_ref, hg_ref, cnew_ref, hlast_ref, cp_sc, hc_sc, *, ns, L, start_pos):
    t_blk = pl.program_id(1)
    rows = ns * L
    C = xl_ref.shape[-1]

    @pl.when(t_blk == 0)
    def _():
        cp_sc[...] = cprev_ref[...]
        hc_sc[...] = h0_ref[...]

    x = xl_ref[...]
    prev = cp_sc[...]

    t8 = lax.broadcasted_iota(jnp.int32, (ns, SUBLANES, C), 1)

    def shifted(k):
        r3 = pltpu.roll(x, k, axis=0).reshape(ns, L, C)
        head = r3[:, :SUBLANES, :]
        for t0 in range(k):
            fill = jnp.broadcast_to(prev[:, CONV_W - 1 - k + t0:CONV_W - k + t0, :], head.shape)
            head = jnp.where(t8 == t0, fill, head)
        r3 = head if L == SUBLANES else jnp.concatenate([head, r3[:, SUBLANES:, :]], axis=1)
        return r3.reshape(rows, C)

    cw = cw_ref[...]
    xc = cb_ref[...] + x * cw[3:4, :]
    for k in range(1, CONV_W):
        xc = xc + shifted(k) * cw[CONV_W - 1 - k:CONV_W - k, :]

    r_parts, i_parts = [], []
    for c in range(C // V7X_MXU_DIM):
        xb = xc[:, c * V7X_MXU_DIM:(c + 1) * V7X_MXU_DIM].astype(BF16)
        r_parts.append(jnp.dot(xb, wrg_ref[c], preferred_element_type=F32))
        i_parts.append(jnp.dot(xb, wig_ref[c], preferred_element_type=F32))
    r = jax.nn.sigmoid(jnp.concatenate(r_parts, axis=1) + brg_ref[...])
    ig = jax.nn.sigmoid(jnp.concatenate(i_parts, axis=1) + big_ref[...])

    nl = -lam_ref[...]
    softplus = jnp.maximum(nl, 0.0) + jnp.log1p(jnp.exp(-jnp.abs(nl)))
    log_a = (-LRU_C) * r * softplus
    a = jnp.exp(log_a)
    z = -jnp.tanh(log_a) * (a * a + 1.0)
    mult = jnp.where(z > 0.0, z * lax.rsqrt(z), 0.0)
    b = mult * (ig * xc)

    nt = L // SUBLANES
    a4 = a.reshape(ns, nt, SUBLANES, C)
    b4 = b.reshape(ns, nt, SUBLANES, C)
    first = t8 == 0
    a0, b0 = a4[:, 0], b4[:, 0]
    if start_pos == 0:
        reset = jnp.logical_and(first, t_blk == 0)
        gx = (ig * xc).reshape(ns, nt, SUBLANES, C)[:, 0]
        a0 = jnp.where(reset, 0.0, a0)
        b0 = jnp.where(reset, gx, b0)
    b0 = b0 + jnp.where(first, a0 * jnp.broadcast_to(hc_sc[...], a0.shape), 0.0)
    if nt > 1:
        a4 = jnp.concatenate([a0[:, None], a4[:, 1:]], axis=1)
        b4 = jnp.concatenate([b0[:, None], b4[:, 1:]], axis=1)
    else:
        a4, b4 = a0[:, None], b0[:, None]

    a3 = a4.reshape(ns * nt, SUBLANES, C)
    b3 = b4.reshape(ns * nt, SUBLANES, C)
    tin = lax.broadcasted_iota(jnp.int32, (ns * nt, SUBLANES, C), 1)
    for d in (1, 2, 4):
        valid = tin >= d
        b3 = b3 + jnp.where(valid, a3, 0.0) * pltpu.roll(b3, d, axis=1)
        a3 = jnp.where(valid, a3 * pltpu.roll(a3, d, axis=1), a3)
    if nt == 1:
        h = b3.reshape(rows, C)
    else:
        a4 = a3.reshape(ns, nt, SUBLANES, C)
        b4 = b3.reshape(ns, nt, SUBLANES, C)
        tiles = [b4[:, 0]]
        for kt in range(1, nt):
            carry = jnp.broadcast_to(tiles[-1][:, SUBLANES - 1:SUBLANES, :], (ns, SUBLANES, C))
            tiles.append(a4[:, kt] * carry + b4[:, kt])
        h = jnp.stack(tiles, axis=1).reshape(rows, C)

    hg_ref[...] = (h * jax.nn.gelu(gl_ref[...])).astype(hg_ref.dtype)
    h3 = h.reshape(ns, L, C)
    x3 = x.reshape(ns, L, C)
    hl = h3[:, L - 1:L, :]
    tail = x3[:, L - (CONV_W - 1):L, :]
    hc_sc[...] = hl
    cp_sc[...] = tail
    hlast_ref[...] = hl
    cnew_ref[...] = tail


def _lru(xl, gl, conv_prev, h0, cw, cb, wrg, brg, wig, big, lam, *, ns, L, start_pos, name):
    NS = conv_prev.shape[0]
    R, C = xl.shape
    T = R // NS
    nt = T // L
    rows = ns * L
    est = 2 * 2 * rows * C * 4 + 2 * rows * C * 2 + 12 * rows * C * 4 + 2 * 4 * 256 * 256 * 2
    row_spec = pl.BlockSpec((rows, C), lambda s, t: (s * nt + t, 0))
    vec = pl.BlockSpec((1, C), lambda s, t: (0, 0))
    return pl.pallas_call(
        functools.partial(_lru_kernel, ns=ns, L=L, start_pos=start_pos),
        out_shape=[jax.ShapeDtypeStruct((R, C), BF16),
                   jax.ShapeDtypeStruct((NS, CONV_W - 1, C), F32),
                   jax.ShapeDtypeStruct((NS, 1, C), F32)],
        grid=(NS // ns, nt),
        in_specs=[row_spec, row_spec,
                  pl.BlockSpec((ns, CONV_W - 1, C), lambda s, t: (s, 0, 0)),
                  pl.BlockSpec((ns, 1, C), lambda s, t: (s, 0, 0)),
                  pl.BlockSpec((CONV_W, C), lambda s, t: (0, 0)), vec,
                  pl.BlockSpec(wrg.shape, lambda s, t: (0, 0, 0)), vec,
                  pl.BlockSpec(wig.shape, lambda s, t: (0, 0, 0)), vec, vec],
        out_specs=[row_spec,
                   pl.BlockSpec((ns, CONV_W - 1, C), lambda s, t: (s, 0, 0)),
                   pl.BlockSpec((ns, 1, C), lambda s, t: (s, 0, 0))],
        scratch_shapes=[pltpu.VMEM((ns, CONV_W - 1, C), F32), pltpu.VMEM((ns, 1, C), F32)],
        compiler_params=pltpu.CompilerParams(dimension_semantics=("arbitrary", "arbitrary"),
                                             vmem_limit_bytes=_vmem_limit(est)),
        name=name,
    )(xl, gl, conv_prev, h0, cw, cb, wrg, brg, wig, big, lam)


def _diff_lambda(lamp, lam_init):
    l1 = jnp.sum(lamp[0:1, :] * lamp[1:2, :], axis=-1, keepdims=True)
    l2 = jnp.sum(lamp[2:3, :] * lamp[3:4, :], axis=-1, keepdims=True)
    return jnp.exp(l1) - jnp.exp(l2) + lam_init


def _stack_maps(q):
    lane = lax.broadcasted_iota(jnp.int32, q.shape, 1)
    return jnp.concatenate([jnp.where(lane < HEAD_DIM, q, 0.0), jnp.where(lane >= HEAD_DIM, q, 0.0)], axis=0)


POS_DIGITS = (256, 16, 1)
ATTN_COL_CHUNK = V7X_MXU_DIM
VT_ROWS = V_DIM + 16


def _attn_prompt_kernel(slopes_ref, q_ref, k_ref, v_ref, lamp_ref, subln_ref, o_ref,
                        qa_sc, ka_sc, vt_sc, m_sc, acc_sc, *, tq, hp, lam_init):
    i = pl.program_id(2)
    tk = tq
    S = k_ref.shape[0]
    dk = 2 * HEAD_DIM

    @pl.when(i == 0)
    def _():
        pos = lax.broadcasted_iota(jnp.int32, (S, dk), 0)
        lane = lax.broadcasted_iota(jnp.int32, (S, dk), 1)
        digit = jnp.where(lane == 0, pos >> 8, jnp.where(lane == 1, (pos >> 4) & 15, jnp.where(lane == 2, pos & 15, 0)))
        digit = digit.astype(F32).astype(BF16)
        for g in range(hp):
            ka_sc[g, :, :dk] = k_ref[:, g * dk:(g + 1) * dk].astype(BF16)
            ka_sc[g, :, dk:] = digit
            vt = v_ref[:, g * V_DIM:(g + 1) * V_DIM].T
            for jb in range(S // tk):
                vt_sc[g, jb, :V_DIM, :] = vt[:, jb * tk:(jb + 1) * tk].astype(BF16)
                vt_sc[g, jb, V_DIM:, :] = jnp.ones((VT_ROWS - V_DIM, tk), BF16)

    lane = lax.broadcasted_iota(jnp.int32, (2 * tq, dk), 1)
    for g in range(hp):
        slope = slopes_ref[pl.program_id(1) * hp + g]
        qa_sc[g, :, :dk] = _stack_maps(q_ref[:, g * dk:(g + 1) * dk].astype(F32) * SCALE).astype(BF16)
        posw = jnp.where(lane == 0, POS_DIGITS[0] * slope,
                         jnp.where(lane == 1, POS_DIGITS[1] * slope, jnp.where(lane == 2, POS_DIGITS[2] * slope, 0.0)))
        qa_sc[g, :, dk:] = posw.astype(BF16)
    m_sc[...] = jnp.full(m_sc.shape, -jnp.inf, F32)
    acc_sc[...] = jnp.zeros(acc_sc.shape, F32)

    def kv_block(j, diag):
        start = pl.multiple_of(j * tk, tk)
        chains = [(g, c0) for g in range(hp) for c0 in range(0, 2 * tq, ATTN_COL_CHUNK)]
        probs = []
        nkeys = lambda c0: (c0 & (tq - 1)) + ATTN_COL_CHUNK if diag else tk
        for g, c0 in chains:
            cols = slice(c0, c0 + ATTN_COL_CHUNK)
            s = lax.dot_general(ka_sc[g, pl.ds(start, nkeys(c0)), :], qa_sc[g, cols, :], NT_DIMS,
                                preferred_element_type=F32)
            if diag:
                qcol = (lax.broadcasted_iota(jnp.int32, s.shape, 1) + c0) & (tq - 1)
                s = jnp.where(qcol >= lax.broadcasted_iota(jnp.int32, s.shape, 0), s, -jnp.inf)
            m_old = m_sc[g, :, cols]
            m_new = jnp.maximum(m_old, jnp.max(s, axis=0, keepdims=True))
            m_sc[g, :, cols] = m_new
            probs.append((jnp.exp(s - m_new).astype(BF16), jnp.exp(m_old - m_new)))
        for (g, c0), (p, corr) in zip(chains, probs):
            cols = slice(c0, c0 + ATTN_COL_CHUNK)
            pv = jnp.dot(vt_sc[g, j, :, :nkeys(c0)], p, preferred_element_type=F32)
            acc_sc[g, :, cols] = acc_sc[g, :, cols] * corr + pv

    def body(j, carry):
        kv_block(j, False)
        return carry

    lax.fori_loop(0, i, body, 0)
    kv_block(i, True)

    lam = _diff_lambda(lamp_ref[...], lam_init)
    for g in range(hp):
        acc = acc_sc[g]
        o = acc[:V_DIM, :] / acc[V_DIM:V_DIM + 1, :]
        od = o[:, :tq] - lam * o[:, tq:]
        y = od * lax.rsqrt(jnp.mean(od * od, axis=0, keepdims=True) + EPS) * subln_ref[...] * (1.0 - lam_init)
        o_ref[:, g * V_DIM:(g + 1) * V_DIM] = y.T.astype(o_ref.dtype)


def _attn_prompt(q, k, v, slopes, lamp, subln_col, *, lam_init, tq, hp, name, kv_batch_off=0):
    B, S, _ = q.shape
    assert S <= 256 * POS_DIGITS[0] and (2 * tq) % ATTN_COL_CHUNK == 0 and S % tq == 0 and N_HEADS % hp == 0
    est = (2 * 2 * S * hp * 128 * 4 + hp * (S * 256 * 2 + S * VT_ROWS * 2 + 2 * tq * 256 * 2 + VT_ROWS * 2 * tq * 4)
           + 2 * hp * tq * 2 * tq * 4 + S * 128 * 4)
    return pl.pallas_call(
        functools.partial(_attn_prompt_kernel, tq=tq, hp=hp, lam_init=lam_init),
        out_shape=jax.ShapeDtypeStruct((B, S, N_HEADS * V_DIM), BF16),
        grid=(B, N_HEADS // hp, S // tq),
        in_specs=[pl.BlockSpec(memory_space=pltpu.SMEM),
                  pl.BlockSpec((None, tq, hp * 2 * HEAD_DIM), lambda b, h, i: (b, i, h)),
                  pl.BlockSpec((None, S, hp * 2 * HEAD_DIM), lambda b, h, i: (b + kv_batch_off, 0, h)),
                  pl.BlockSpec((None, S, hp * V_DIM), lambda b, h, i: (b + kv_batch_off, 0, h)),
                  pl.BlockSpec((4, HEAD_DIM), lambda b, h, i: (0, 0)),
                  pl.BlockSpec((V_DIM, 1), lambda b, h, i: (0, 0))],
        out_specs=pl.BlockSpec((None, tq, hp * V_DIM), lambda b, h, i: (b, i, h)),
        scratch_shapes=[pltpu.VMEM((hp, 2 * tq, 4 * HEAD_DIM), BF16),
                        pltpu.VMEM((hp, S, 4 * HEAD_DIM), BF16), pltpu.VMEM((hp, S // tq, VT_ROWS, tq), BF16),
                        pltpu.VMEM((hp, 1, 2 * tq), F32), pltpu.VMEM((hp, VT_ROWS, 2 * tq), F32)],
        compiler_params=pltpu.CompilerParams(dimension_semantics=("arbitrary", "arbitrary", "arbitrary"),
                                             vmem_limit_bytes=_vmem_limit(est)),
        name=name,
    )(slopes, q, k, v, lamp, subln_col)


def _attn_sample_kernel(pt_ref, q_ref, k_ref, v_ref, slope_ref, lamp_ref, subln_ref, *rest,
                        pb, ds, past_len, lam_init):
    kpages, vpages = rest[:pb], rest[pb:2 * pb]
    o_ref, qs_sc, ko_sc, vo_sc, m_sc, l_sc, acc_sc = rest[2 * pb:]
    p = pl.program_id(1)
    hh = N_HEADS // 2
    nq = 2 * ds
    nrow = hh * nq
    ncol = 2 * PAGE_SIZE
    row = lax.broadcasted_iota(jnp.int32, (nrow, ncol), 0)
    col = lax.broadcasted_iota(jnp.int32, (nrow, ncol), 1)
    even = (col & 1) == 0
    E, O = slice(0, nrow), slice(nrow, 2 * nrow)
    bcast = lambda a, shape=(nrow, ncol): jnp.broadcast_to(a, shape)
    slope = jnp.where(even, bcast(slope_ref[E, :]), bcast(slope_ref[O, :]))
    dq = (row & (ds - 1)) - (col >> 1)
    nsd = (-slope) * dq.astype(F32)

    def pair_rows(ref, a):
        return ref[pl.ds(a, ncol, stride=hh), :].astype(BF16)

    def update(krefs, vrefs, biases):
        n = len(krefs)
        ev = (lax.broadcasted_iota(jnp.int32, (nrow, n * ncol), 1) & 1) == 0
        ev_head = (lax.broadcasted_iota(jnp.int32, (nq, n * ncol), 1) & 1) == 0
        s_pairs = []
        for a in range(hh):
            ka = jnp.concatenate([pair_rows(r, a) for r in krefs], axis=0)
            s2 = lax.dot_general(qs_sc[a * 2 * nq:(a + 1) * 2 * nq, :], ka, NT_DIMS, preferred_element_type=F32)
            s_pairs.append(jnp.where(ev_head, s2[:nq], s2[nq:]))
        s = jnp.concatenate(s_pairs, axis=0) + jnp.concatenate(biases, axis=1)
        m_old_e, m_old_o = m_sc[E, :], m_sc[O, :]
        m_e = jnp.maximum(m_old_e, jnp.max(jnp.where(ev, s, -jnp.inf), axis=1, keepdims=True))
        m_o = jnp.maximum(m_old_o, jnp.max(jnp.where(ev, -jnp.inf, s), axis=1, keepdims=True))
        pr = jnp.exp(s - jnp.where(ev, bcast(m_e, s.shape), bcast(m_o, s.shape)))
        pe = jnp.where(ev, pr, 0.0)
        po = jnp.where(ev, 0.0, pr)
        corr_e = jnp.exp(m_old_e - m_e)
        corr_o = jnp.exp(m_old_o - m_o)
        l_sc[E, :] = l_sc[E, :] * corr_e + jnp.sum(pe, axis=1, keepdims=True)
        l_sc[O, :] = l_sc[O, :] * corr_o + jnp.sum(po, axis=1, keepdims=True)
        for a in range(hh):
            va = jnp.concatenate([pair_rows(r, a) for r in vrefs], axis=0)
            rs = slice(a * nq, (a + 1) * nq)
            p2 = jnp.concatenate([pe[rs], po[rs]], axis=0).astype(BF16)
            pv = jnp.dot(p2, va, preferred_element_type=F32)
            acc_sc[a, 0] = acc_sc[a, 0] * corr_e[rs] + pv[:nq]
            acc_sc[a, 1] = acc_sc[a, 1] * corr_o[rs] + pv[nq:]
        m_sc[E, :] = m_e
        m_sc[O, :] = m_o

    @pl.when(p == 0)
    def _():
        q = q_ref[...].astype(F32) * SCALE
        k_own, v_own = k_ref[...], v_ref[...]
        ko_sc[...] = jnp.zeros(ko_sc.shape, F32)
        vo_sc[...] = jnp.zeros(vo_sc.shape, F32)
        for hd in range(N_HEADS):
            cols = slice(hd * V_DIM, (hd + 1) * V_DIM)
            r0 = (hd % hh) * 2 * nq + (hd // hh) * nq
            qs_sc[r0:r0 + nq, :] = _stack_maps(q[:, cols]).astype(BF16)
            ko_sc[pl.ds(hd, ds, stride=N_HEADS), :] = k_own[:, cols]
            vo_sc[pl.ds(hd, ds, stride=N_HEADS), :] = v_own[:, cols]
        m_sc[...] = jnp.full(m_sc.shape, -jnp.inf, F32)
        l_sc[...] = jnp.zeros(l_sc.shape, F32)
        acc_sc[...] = jnp.zeros(acc_sc.shape, F32)
        update([ko_sc], [vo_sc], [jnp.where(dq >= 0, nsd, -jnp.inf)])

    dist0 = [(past_len - (p * pb + ip) * PAGE_SIZE).astype(F32) for ip in range(pb)]
    update(kpages, vpages, [nsd - slope * d0 for d0 in dist0])

    @pl.when(p == pl.num_programs(1) - 1)
    def _():
        lam = _diff_lambda(lamp_ref[...], lam_init)
        for hd in range(N_HEADS):
            a, member = hd % hh, hd // hh
            o = acc_sc[a, member] / l_sc[member * nrow + a * nq:member * nrow + (a + 1) * nq, :]
            od = o[:ds, :] - lam * o[ds:, :]
            o_ref[:, hd * V_DIM:(hd + 1) * V_DIM] = (_rms(od, subln_ref[...]) * (1.0 - lam_init)).astype(o_ref.dtype)


def _attn_sample(q, k, v, cache_k, cache_v, page_table, slope_pairs, lamp, subln, *, layer, lam_init, pb, name):
    DB, DS, HV = q.shape
    n_pages = page_table.shape[1]
    assert n_pages % pb == 0 and N_HEADS % 2 == 0
    nrow = (N_HEADS // 2) * 2 * DS
    prow = PAGE_SIZE * N_HEADS
    tok_spec = pl.BlockSpec((None, DS, HV), lambda b, p, pt: (b, 0, 0))

    def page_spec(ip):
        return pl.BlockSpec((None, None, prow, V_DIM), lambda b, p, pt: (layer, pt[b, p * pb + ip], 0, 0))

    est = 2 * 2 * pb * PAGE_SIZE * HV * 4 + 2 * PAGE_SIZE * HV * 4 + 16 * nrow * pb * 2 * PAGE_SIZE * 4
    grid_spec = pltpu.PrefetchScalarGridSpec(
        num_scalar_prefetch=1,
        grid=(DB, n_pages // pb),
        in_specs=[tok_spec, tok_spec, tok_spec,
                  pl.BlockSpec((2 * nrow, 1), lambda b, p, pt: (0, 0)),
                  pl.BlockSpec((4, HEAD_DIM), lambda b, p, pt: (0, 0)),
                  pl.BlockSpec((1, V_DIM), lambda b, p, pt: (0, 0))]
                 + [page_spec(ip) for ip in range(pb)] * 2,
        out_specs=tok_spec,
        scratch_shapes=[pltpu.VMEM((2 * nrow, V_DIM), BF16),
                        pltpu.VMEM((prow, V_DIM), F32), pltpu.VMEM((prow, V_DIM), F32),
                        pltpu.VMEM((2 * nrow, 1), F32), pltpu.VMEM((2 * nrow, 1), F32),
                        pltpu.VMEM((N_HEADS // 2, 2, 2 * DS, V_DIM), F32)])
    return pl.pallas_call(
        functools.partial(_attn_sample_kernel, pb=pb, ds=DS, past_len=n_pages * PAGE_SIZE, lam_init=lam_init),
        out_shape=jax.ShapeDtypeStruct((DB, DS, HV), F32),
        grid_spec=grid_spec,
        compiler_params=pltpu.CompilerParams(dimension_semantics=("arbitrary", "arbitrary"),
                                             vmem_limit_bytes=_vmem_limit(est)),
        name=name,
    )(page_table, q, k, v, slope_pairs, lamp, subln, *([cache_k] * pb), *([cache_v] * pb))


def _mix_kernel(hg_ref, o_ref, gl_ref, ga_ref, x_ref, wbl_ref, wba_ref, wout_ref, y_ref):
    bl = jnp.dot(hg_ref[...].astype(BF16), wbl_ref[...], preferred_element_type=F32)
    ba = jnp.dot(o_ref[...].astype(BF16), wba_ref[...], preferred_element_type=F32)
    mixed = jax.nn.sigmoid(gl_ref[...]) * bl + jax.nn.sigmoid(ga_ref[...]) * ba
    y_ref[...] = x_ref[...] + jnp.dot(mixed.astype(BF16), wout_ref[...], preferred_element_type=F32)


def _mix(hg, o, gl, ga, x, wbl, wba, wout, *, tm, name):
    R, D = x.shape
    row = lambda a: pl.BlockSpec((tm, a.shape[1]), lambda i: (i, 0))
    est = 3 * D * D * 2 + 2 * tm * D * (2 + 4 + 4 + 4 + 4 + 4) + 4 * tm * D * 4
    return pl.pallas_call(
        _mix_kernel,
        out_shape=jax.ShapeDtypeStruct((R, D), F32),
        grid=(R // tm,),
        in_specs=[row(hg), row(o), row(gl), row(ga), row(x), _full(wbl.shape), _full(wba.shape), _full(wout.shape)],
        out_specs=row(x),
        compiler_params=pltpu.CompilerParams(dimension_semantics=("arbitrary",), vmem_limit_bytes=_vmem_limit(est)),
        name=name,
    )(hg, o, gl, ga, x, wbl, wba, wout)


def _mem_attn_kernel(x_ref, g_ref, wq_ref, mk_ref, mv_ref, wo_ref, y_ref):
    x = x_ref[...]
    xn = _rms(x, g_ref[...]).astype(BF16)
    q = (jnp.dot(xn, wq_ref[...], preferred_element_type=F32) * (MEM_HEAD_DIM ** -0.5)).astype(BF16)
    mk = mk_ref[...].astype(BF16)
    mv = mv_ref[...].astype(BF16)
    outs = []
    for hd in range(MEM_HEADS):
        sl = slice(hd * MEM_HEAD_DIM, (hd + 1) * MEM_HEAD_DIM)
        s = lax.dot_general(q[:, sl], mk[:, sl], NT_DIMS, preferred_element_type=F32)
        e = jnp.exp(s - jnp.max(s, axis=-1, keepdims=True))
        pr = e / jnp.sum(e, axis=-1, keepdims=True)
        outs.append(jnp.dot(pr.astype(BF16), mv[:, sl], preferred_element_type=F32))
    o = jnp.concatenate(outs, axis=1).astype(BF16)
    y_ref[...] = x + jnp.dot(o, wo_ref[...], preferred_element_type=F32)


def _mem_attn(x, g, wq, mk, mv, wo, *, tm, name):
    NB, T, D = x.shape
    n_mem = mk.shape[1]
    xs = pl.BlockSpec((None, tm, D), lambda b, i: (b, i, 0))
    ms = pl.BlockSpec((None, n_mem, D_MEM), lambda b, i: (b, 0, 0))
    est = 4 * tm * D * 4 + 4 * n_mem * D_MEM * 4 + 2 * D * D_MEM * 2 + 8 * tm * n_mem * 4
    return pl.pallas_call(
        _mem_attn_kernel,
        out_shape=jax.ShapeDtypeStruct((NB, T, D), F32),
        grid=(NB, T // tm),
        in_specs=[xs, pl.BlockSpec((1, D), lambda b, i: (0, 0)), _full(wq.shape), ms, ms, _full(wo.shape)],
        out_specs=xs,
        compiler_params=pltpu.CompilerParams(dimension_semantics=("arbitrary", "arbitrary"),
                                             vmem_limit_bytes=_vmem_limit(est)),
        name=name,
    )(x, g, wq, mk, mv, wo)


def _ffn_kernel(x_ref, g_ref, wg_ref, wu_ref, wd_ref, gf_ref, y_ref, *, final_norm):
    x = x_ref[...]
    xn = _rms(x, g_ref[...]).astype(BF16)
    d_ff = wg_ref.shape[1]
    acc = x
    for c in range(d_ff // V7X_MXU_DIM):
        sl = slice(c * V7X_MXU_DIM, (c + 1) * V7X_MXU_DIM)
        gate = jnp.dot(xn, wg_ref[:, sl], preferred_element_type=F32)
        up = jnp.dot(xn, wu_ref[:, sl], preferred_element_type=F32)
        acc = acc + jnp.dot((jax.nn.silu(gate) * up).astype(BF16), wd_ref[sl, :], preferred_element_type=F32)
    y_ref[...] = _rms(acc, gf_ref[...]) if final_norm else acc


def _ffn(x, g, wg, wu, wd, gf, *, final_norm, tm, name):
    R, D = x.shape
    row = pl.BlockSpec((tm, D), lambda i: (i, 0))
    vec = pl.BlockSpec((1, D), lambda i: (0, 0))
    est = 3 * D * wg.shape[1] * 2 + 4 * tm * D * 4 + 6 * tm * D * 4
    return pl.pallas_call(
        functools.partial(_ffn_kernel, final_norm=final_norm),
        out_shape=jax.ShapeDtypeStruct((R, D), F32),
        grid=(R // tm,),
        in_specs=[row, vec, _full(wg.shape), _full(wu.shape), _full(wd.shape), vec],
        out_specs=row,
        compiler_params=pltpu.CompilerParams(dimension_semantics=("arbitrary",), vmem_limit_bytes=_vmem_limit(est)),
        name=name,
    )(x, g, wg, wu, wd, gf)


def _block_diag_tiles(w):
    nb = w.shape[0]
    per = V7X_MXU_DIM // LRU_BLOCK
    eye = jnp.eye(per, dtype=w.dtype)
    w4 = w.reshape(nb // per, per, LRU_BLOCK, LRU_BLOCK)
    t = jnp.einsum('gakj,ab->gakbj', w4, eye)
    return t.reshape(nb // per, V7X_MXU_DIM, V7X_MXU_DIM).astype(BF16)


def _row(v):
    return v.reshape(1, -1).astype(F32)


def kernel(x_prompt, mem_prompt, x_sample, cache_k, cache_v, page_table, cache_mem_k, cache_mem_v, state_conv, state_h, norm_mix, w_in, conv_w, conv_b, w_rg, b_rg, w_ig, b_ig, lru_lambda, lam_q1, lam_k1, lam_q2, lam_k2, subln, w_br_lru, w_br_attn, w_out, norm_mem, norm_memkv, w_mem_q, w_mem_kv, w_mem_o, norm_ffn, w_gate_up, w_down, norm_final):
    depth = w_in.shape[0]
    B, S, D = x_prompt.shape
    DB, DS, _ = x_sample.shape
    n_mem = mem_prompt.shape[1]
    n_pool = cache_k.shape[1]
    n_pages = page_table.shape[1]
    d_ff = w_down.shape[1]
    HV = N_HEADS * V_DIM

    slopes_np = np.array([2.0 ** (-8.0 * (h + 1) / N_HEADS) for h in range(N_HEADS)], dtype=np.float32)
    slopes = jnp.asarray(slopes_np)
    slope_pairs = jnp.asarray(np.repeat(slopes_np, 2 * DS).reshape(-1, 1))
    ck = cache_k.reshape(depth, n_pool, PAGE_SIZE * N_HEADS, V_DIM)
    cv = cache_v.reshape(depth, n_pool, PAGE_SIZE * N_HEADS, V_DIM)
    mem2d = mem_prompt.reshape(B * n_mem, D)
    gfin = _row(norm_final)

    xp = x_prompt.reshape(B * S, D)
    xs = x_sample.reshape(DB * DS, D)
    zc_p = jnp.zeros((B, CONV_W - 1, D_RNN), F32)
    zh_p = jnp.zeros((B, 1, D_RNN), F32)
    kv_p = ()
    outs_p = {n: [] for n in ('mk', 'mv', 'c', 'h')}
    outs_s = {n: [] for n in ('k', 'v', 'c', 'h')}

    for layer in range(depth):
        lam_init = 0.8 - 0.6 * math.exp(-0.3 * layer)
        last = layer == depth - 1
        w_in_b = w_in[layer].astype(BF16)
        wrg = _block_diag_tiles(w_rg[layer])
        wig = _block_diag_tiles(w_ig[layer])
        lamp = jnp.stack([lam_q1[layer], lam_k1[layer], lam_q2[layer], lam_k2[layer]]).astype(F32)
        wbl, wba, wout = (w_br_lru[layer].astype(BF16), w_br_attn[layer].astype(BF16), w_out[layer].astype(BF16))
        wmq, wmo = w_mem_q[layer].astype(BF16), w_mem_o[layer].astype(BF16)
        wgu = w_gate_up[layer].astype(BF16)
        wg, wu, wd = wgu[:, :d_ff], wgu[:, d_ff:], w_down[layer].astype(BF16)
        lru_args = (conv_w[layer].astype(F32), _row(conv_b[layer]), wrg, _row(b_rg[layer]), wig, _row(b_ig[layer]),
                    _row(lru_lambda[layer]))

        def trunk(x2d, nb, t, mk, mv, conv_prev, h0, start_pos, attn_fn, tm, tm_mem, ns, L, tag, kv_prev=None):
            stack_onto = {3: kv_prev[0], 4: kv_prev[1]} if kv_prev else None
            xl, gl, q, k, v, gtl, gta = _norm_proj(x2d, _row(norm_mix[layer]), w_in_b,
                                                   (F32, F32, BF16, F32, F32, F32, F32), min(tm, 256), f"in_proj_{tag}",
                                                   stack_onto=stack_onto)
            hg, c_new, h_last = _lru(xl, gl, conv_prev, h0, *lru_args, ns=ns, L=L, start_pos=start_pos,
                                     name=f"lru_{tag}")
            q3 = q.reshape(nb, t, HV)
            if kv_prev is None:
                o = attn_fn(q3, k.reshape(nb, t, HV), v.reshape(nb, t, HV))
            else:
                k, v = k.reshape(-1, nb * t, HV), v.reshape(-1, nb * t, HV)
                o = attn_fn(q3, k.reshape(-1, t, HV), v.reshape(-1, t, HV), kv_batch_off=(k.shape[0] - 1) * nb)
            x2 = _mix(hg, o.reshape(nb * t, HV), gtl, gta, x2d, wbl, wba, wout, tm=tm, name=f"mix_{tag}")
            x3 = _mem_attn(x2.reshape(nb, t, D), _row(norm_mem[layer]), wmq, mk, mv, wmo, tm=tm_mem,
                           name=f"mem_attn_{tag}").reshape(nb * t, D)
            x4 = _ffn(x3, _row(norm_ffn[layer]), wg, wu, wd, gfin, final_norm=last, tm=tm, name=f"ffn_{tag}")
            return x4, c_new, h_last.reshape(nb, D_RNN), k, v

        mk2, mv2 = _norm_proj(mem2d, _row(norm_memkv[layer]), w_mem_kv[layer].astype(BF16), (F32, F32), 256,
                              f"mem_kv_{layer}")
        mk_p = mk2.reshape(B, n_mem, D_MEM)
        mv_p = mv2.reshape(B, n_mem, D_MEM)
        attn_p = functools.partial(_attn_prompt, slopes=slopes, lamp=lamp,
                                   subln_col=subln[layer].reshape(V_DIM, 1).astype(F32),
                                   lam_init=lam_init, tq=min(S, 1024), hp=2, name=f"attn_prompt_{layer}")
        xp, c_new, h_last, kst_p, vst_p = trunk(xp, B, S, mk_p, mv_p, zc_p, zh_p, 0, attn_p, 512, 512, 1, 256,
                                                f"p{layer}", kv_prev=kv_p)
        kv_p = (kst_p, vst_p)
        outs_p['mk'].append(mk_p.reshape(B, n_mem, MEM_HEADS, MEM_HEAD_DIM))
        outs_p['mv'].append(mv_p.reshape(B, n_mem, MEM_HEADS, MEM_HEAD_DIM))
        outs_p['c'].append(c_new); outs_p['h'].append(h_last)

        attn_s = functools.partial(_attn_sample, cache_k=ck, cache_v=cv, page_table=page_table,
                                   slope_pairs=slope_pairs, lamp=lamp, subln=_row(subln[layer]), layer=layer,
                                   lam_init=lam_init, pb=min(n_pages, 16), name=f"attn_sample_{layer}")
        xs, c_new, h_last, k, v = trunk(xs, DB, DS, cache_mem_k[layer].reshape(DB, n_mem, D_MEM),
                                        cache_mem_v[layer].reshape(DB, n_mem, D_MEM), state_conv[layer],
                                        state_h[layer].reshape(DB, 1, D_RNN), n_pages * PAGE_SIZE, attn_s,
                                        DB * DS, DS, DB, DS, f"s{layer}")
        outs_s['k'].append(k.reshape(DB, DS, N_HEADS, V_DIM)); outs_s['v'].append(v.reshape(DB, DS, N_HEADS, V_DIM))
        outs_s['c'].append(c_new); outs_s['h'].append(h_last)

    st = jnp.stack
    return (xp.reshape(B, S, D), xs.reshape(DB, DS, D),
            kv_p[0].reshape(depth, B, S, N_HEADS, V_DIM), kv_p[1].reshape(depth, B, S, N_HEADS, V_DIM),
            st(outs_p['mk']), st(outs_p['mv']), st(outs_p['c']), st(outs_p['h']),
            st(outs_s['k']), st(outs_s['v']), st(outs_s['c']), st(outs_s['h']))
```
